```python
import math
import jax
import jax.numpy as jnp
from jax import lax
import numpy as np


D_MODEL = 2048
BATCH = 16
SEQ = 2048
DEPTH = 1
DEC_BATCH = 32
DEC_SEQ = 1
PAST_LEN = 16384
PAGE_SIZE = 128

HEAD_DIM = 128
N_META = 16
DIFF_HEADS = D_MODEL // (4 * HEAD_DIM)
DIFF_QK = HEAD_DIM
DIFF_V = 2 * HEAD_DIM
FOX_HEADS = D_MODEL // (2 * HEAD_DIM)
FOX_DIM = HEAD_DIM
MIX_WIDTH = DIFF_HEADS * DIFF_V + FOX_HEADS * FOX_DIM
D_FF = ((8 * D_MODEL + 3 * 256 - 1) // (3 * 256)) * 256
QBLOCK = 128
ROPE_THETA = 10000.0
EPS = 1e-6
DIFF_SCALE = DIFF_QK ** -0.5
FOX_SCALE = FOX_DIM ** -0.5
SPLIT_SIZES = (DIFF_HEADS * 2 * DIFF_QK, DIFF_HEADS * 2 * DIFF_QK, DIFF_HEADS * DIFF_V,
               FOX_HEADS * FOX_DIM, FOX_HEADS * FOX_DIM, FOX_HEADS * FOX_DIM, FOX_HEADS)
IN_COLS = sum(SPLIT_SIZES)

kernel_name = 'hybrid_diffattn_fox_meta_decode_step'


def rms_norm(x, g):
    xf = x.astype(jnp.float32)
    y = xf * lax.rsqrt(jnp.mean(xf * xf, axis=-1, keepdims=True) + EPS)
    return (y * g.astype(jnp.float32)).astype(x.dtype)


def rope(x, pos):
    half = x.shape[-1] // 2
    inv = ROPE_THETA ** (-jnp.arange(half, dtype=jnp.float32) / half)
    ang = pos.astype(jnp.float32)[:, None] * inv[None, :]
    shape = (pos.shape[0],) + (1,) * (x.ndim - 3) + (half,)
    cos = jnp.cos(ang).reshape(shape)
    sin = jnp.sin(ang).reshape(shape)
    xf = x.astype(jnp.float32)
    x1, x2 = xf[..., :half], xf[..., half:]
    return jnp.concatenate([x1 * cos - x2 * sin, x2 * cos + x1 * sin], axis=-1).astype(x.dtype)


def project(h, w_in, b_f, pos):
    B, S, _ = h.shape
    z = jnp.einsum('bsd,de->bse', h, w_in)
    idx = [int(i) for i in np.cumsum(SPLIT_SIZES)[:-1]]
    dq, dk, dv, fq, fk, fv, fl = jnp.split(z, idx, axis=-1)
    dq = rope(dq.reshape(B, S, DIFF_HEADS, 2, DIFF_QK), pos)
    dk = rope(dk.reshape(B, S, DIFF_HEADS, 2, DIFF_QK), pos)
    dv = dv.reshape(B, S, DIFF_HEADS, DIFF_V)
    fq = fq.reshape(B, S, FOX_HEADS, FOX_DIM)
    fk = fk.reshape(B, S, FOX_HEADS, FOX_DIM)
    fv = fv.reshape(B, S, FOX_HEADS, FOX_DIM)
    logf = jax.nn.log_sigmoid(fl.astype(jnp.float32) + b_f.astype(jnp.float32))
    return dq, dk, dv, fq, fk, fv, logf


def diff_lambda_value(lam_params, lam_init):
    lp = lam_params.astype(jnp.float32)
    return jnp.exp(jnp.sum(lp[0] * lp[1])) - jnp.exp(jnp.sum(lp[2] * lp[3])) + lam_init


def attend_block(qd, qf, dq_cum, qpos, kd, vd, kf, vf, dk_cum, kpos, lam):
    vis = kpos[None, :] <= qpos[:, None]
    sd = jnp.einsum('bqhcd,bkhcd->bhcqk', qd, kd).astype(jnp.float32) * DIFF_SCALE
    pd = jax.nn.softmax(jnp.where(vis, sd, -jnp.inf), axis=-1)
    wd = pd[:, :, 0] - lam * pd[:, :, 1]
    od = jnp.einsum('bhqk,bkhe->bqhe', wd.astype(vd.dtype), vd)
    bias = jnp.transpose(dq_cum, (0, 2, 1))[..., :, None] - jnp.transpose(dk_cum, (0, 2, 1))[..., None, :]
    sf = jnp.einsum('bqhd,bkhd->bhqk', qf, kf).astype(jnp.float32) * FOX_SCALE + bias
    pf = jax.nn.softmax(jnp.where(vis, sf, -jnp.inf), axis=-1)
    of = jnp.einsum('bhqk,bkhd->bqhd', pf.astype(vf.dtype), vf)
    return od, of


def prompt_attention(dq, dk, dv, fq, fk, fv, logf, lam):
    B, L = dq.shape[:2]
    S = L - N_META
    nb = S // QBLOCK
    pos = jnp.arange(L)
    dcum = jnp.cumsum(logf, axis=1)
    M = N_META
    od_m, of_m = attend_block(dq[:, :M], fq[:, :M], dcum[:, :M], pos[:M],
                              dk[:, :M], dv[:, :M], fk[:, :M], fv[:, :M], dcum[:, :M], pos[:M], lam)

    def to_blocks(a):
        a = a[:, M:]
        return jnp.moveaxis(a.reshape((B, nb, QBLOCK) + a.shape[2:]), 1, 0)

    def body(blk):
        q_d, q_f, c_q, q_pos = blk
        return attend_block(q_d, q_f, c_q, q_pos, dk, dv, fk, fv, dcum, pos, lam)

    od_r, of_r = lax.map(body, (to_blocks(dq), to_blocks(fq), to_blocks(dcum), pos[M:].reshape(nb, QBLOCK)))

    def from_blocks(a):
        return jnp.moveaxis(a, 0, 1).reshape((B, S) + a.shape[3:])

    od = jnp.concatenate([od_m, from_blocks(od_r)], axis=1)
    of = jnp.concatenate([of_m, from_blocks(of_r)], axis=1)
    return od, of


def online_init(s, v, eq):
    m = jnp.max(s, axis=-1)
    p = jnp.exp(s - m[..., None])
    return (m, jnp.sum(p, axis=-1), jnp.einsum(eq, p, v.astype(jnp.float32)))


def online_update(state, s, v, eq):
    m, l, acc = state
    m_new = jnp.maximum(m, jnp.max(s, axis=-1))
    a = jnp.exp(m - m_new)
    p = jnp.exp(s - m_new[..., None])
    return (m_new, l * a + jnp.sum(p, axis=-1),
            acc * a[..., None] + jnp.einsum(eq, p, v.astype(jnp.float32)))


def sample_attention(dq, dk, dv, fq, fk, fv, logf, lam, cache_diff_kv, cache_fox_kv, cache_fox_logf, page_table, layer):
    DB, T = dq.shape[:2]
    NP = page_table.shape[1]
    P = NP * PAGE_SIZE
    eq_d = 'bhcqk,bkhe->bhcqe'
    eq_f = 'bhqk,bkhd->bhqd'
    c_new = jnp.cumsum(logf, axis=1)
    c_t = jnp.transpose(c_new, (0, 2, 1))[..., :, None]
    vis = jnp.arange(T)[None, :] <= jnp.arange(T)[:, None]
    sd = jnp.einsum('bqhcd,bkhcd->bhcqk', dq, dk).astype(jnp.float32) * DIFF_SCALE
    st_d = online_init(jnp.where(vis, sd, -jnp.inf), dv, eq_d)
    sf = (jnp.einsum('bqhd,bkhd->bhqk', fq, fk).astype(jnp.float32) * FOX_SCALE
          + c_t - jnp.transpose(c_new, (0, 2, 1))[:, :, None, :])
    st_f = online_init(jnp.where(vis, sf, -jnp.inf), fv, eq_f)
    lf_past = cache_fox_logf[layer, page_table].astype(jnp.float32).reshape(DB, P, FOX_HEADS)
    suffix = lax.cumsum(lf_past, axis=1, reverse=True) - lf_past
    suffix = jnp.moveaxis(suffix.reshape(DB, NP, PAGE_SIZE, FOX_HEADS), 1, 0)

    def step(carry, xs):
        s_d, s_f = carry
        pid, suf = xs
        kv_d = cache_diff_kv[layer, pid]
        k_d = kv_d[:, :, 0].reshape(DB, PAGE_SIZE, DIFF_HEADS, 2, DIFF_QK).astype(dq.dtype)
        sc_d = jnp.einsum('bqhcd,bkhcd->bhcqk', dq, k_d).astype(jnp.float32) * DIFF_SCALE
        s_d = online_update(s_d, sc_d, kv_d[:, :, 1], eq_d)
        kv_f = cache_fox_kv[layer, pid]
        sc_f = (jnp.einsum('bqhd,bkhd->bhqk', fq, kv_f[:, :, 0].astype(fq.dtype)).astype(jnp.float32) * FOX_SCALE
                + c_t + jnp.transpose(suf, (0, 2, 1))[:, :, None, :])
        s_f = online_update(s_f, sc_f, kv_f[:, :, 1], eq_f)
        return (s_d, s_f), None

    (st_d, st_f), _ = lax.scan(step, (st_d, st_f), (page_table.T, suffix))
    _, l_d, acc_d = st_d
    o = acc_d / l_d[..., None]
    od = jnp.transpose(o[:, :, 0] - lam * o[:, :, 1], (0, 2, 1, 3))
    _, l_f, acc_f = st_f
    of = jnp.transpose(acc_f / l_f[..., None], (0, 2, 1, 3))
    return od, of


def merge_heads(od, of, g_subln, lam_init, w_o, dtype):
    B, S = od.shape[:2]
    od = rms_norm(od.astype(jnp.float32), g_subln) * (1.0 - lam_init)
    o = jnp.concatenate([od.reshape(B, S, -1).astype(dtype), of.reshape(B, S, -1).astype(dtype)], axis=-1)
    return jnp.einsum('bsm,md->bsd', o, w_o)


def ffn(x, g_pre, w_gate, w_up, w_down, g_post):
    h = rms_norm(x, g_pre)
    u = jax.nn.silu(jnp.einsum('bsd,df->bsf', h, w_gate)) * jnp.einsum('bsd,df->bsf', h, w_up)
    return x + rms_norm(jnp.einsum('bsf,fd->bsd', u, w_down), g_post)


def cache_rows(dk, dv, fk, fv, logf, d_dtype, f_dtype, lf_dtype):
    B, S = dk.shape[:2]
    kv_d = jnp.stack([dk.reshape(B, S, DIFF_HEADS, 2 * DIFF_QK), dv], axis=2).astype(d_dtype)
    kv_f = jnp.stack([fk, fv], axis=2).astype(f_dtype)
    return kv_d, kv_f, logf.astype(lf_dtype)


def setup_inputs(seed: int = 0) -> dict:
    key = jax.random.key(seed)
    ks = jax.random.split(key, 20)
    n_pages = PAST_LEN // PAGE_SIZE
    n_pool = (DEC_BATCH * n_pages * 5) // 4

    def nrm(k, shape, scale=1.0):
        return jax.random.normal(k, shape, jnp.float32) * scale

    x_prompt = nrm(ks[0], (BATCH, SEQ, D_MODEL))
    x_sample = nrm(ks[1], (DEC_BATCH, DEC_SEQ, D_MODEL))
    cache_diff_kv = nrm(ks[2], (DEPTH, n_pool, PAGE_SIZE, 2, DIFF_HEADS, DIFF_V))
    cache_fox_kv = nrm(ks[3], (DEPTH, n_pool, PAGE_SIZE, 2, FOX_HEADS, FOX_DIM))
    cache_fox_logf = jax.nn.log_sigmoid(2.0 + nrm(ks[4], (DEPTH, n_pool, PAGE_SIZE, FOX_HEADS), 0.5))
    page_table = jax.random.permutation(ks[5], n_pool)[: DEC_BATCH * n_pages].reshape(DEC_BATCH, n_pages).astype(jnp.int32)
    meta_tokens = nrm(ks[6], (N_META, D_MODEL))
    g_attn_pre = 1.0 + nrm(ks[7], (DEPTH, D_MODEL), 0.05)
    w_in = nrm(ks[8], (DEPTH, D_MODEL, IN_COLS), D_MODEL ** -0.5)
    b_f = 2.0 + nrm(ks[9], (DEPTH, FOX_HEADS), 0.1)
    diff_lambda = nrm(ks[10], (DEPTH, 4, DIFF_QK), 0.1)
    g_subln = 1.0 + nrm(ks[11], (DEPTH, DIFF_V), 0.05)
    w_o = nrm(ks[12], (DEPTH, MIX_WIDTH, D_MODEL), MIX_WIDTH ** -0.5)
    g_attn_post = 1.0 + nrm(ks[13], (DEPTH, D_MODEL), 0.05)
    g_ffn_pre = 1.0 + nrm(ks[14], (DEPTH, D_MODEL), 0.05)
    w_gate = nrm(ks[15], (DEPTH, D_MODEL, D_FF), D_MODEL ** -0.5)
    w_up = nrm(ks[16], (DEPTH, D_MODEL, D_FF), D_MODEL ** -0.5)
    w_down = nrm(ks[17], (DEPTH, D_FF, D_MODEL), D_FF ** -0.5)
    g_ffn_post = 1.0 + nrm(ks[18], (DEPTH, D_MODEL), 0.05)
    return {'x_prompt': x_prompt, 'x_sample': x_sample,
            'cache_diff_kv': cache_diff_kv, 'cache_fox_kv': cache_fox_kv, 'cache_fox_logf': cache_fox_logf,
            'page_table': page_table, 'meta_tokens': meta_tokens,
            'g_attn_pre': g_attn_pre, 'w_in': w_in, 'b_f': b_f, 'diff_lambda': diff_lambda,
            'g_subln': g_subln, 'w_o': w_o, 'g_attn_post': g_attn_post,
            'g_ffn_pre': g_ffn_pre, 'w_gate': w_gate, 'w_up': w_up, 'w_down': w_down, 'g_ffn_post': g_ffn_post}


def reference(x_prompt, x_sample, cache_diff_kv, cache_fox_kv, cache_fox_logf, page_table, meta_tokens,
              g_attn_pre, w_in, b_f, diff_lambda, g_subln, w_o, g_attn_post,
              g_ffn_pre, w_gate, w_up, w_down, g_ffn_post):
    B, S, Dm = x_prompt.shape
    T = x_sample.shape[1]
    L = N_META + S
    P = page_table.shape[1] * PAGE_SIZE
    pos_p = jnp.arange(L)
    pos_s = P + jnp.arange(T)
    xp = jnp.concatenate([jnp.broadcast_to(meta_tokens.astype(x_prompt.dtype)[None], (B, N_META, Dm)), x_prompt], axis=1)
    xs = x_sample
    dtypes = (cache_diff_kv.dtype, cache_fox_kv.dtype, cache_fox_logf.dtype)
    p_kvd, p_kvf, p_lf = [], [], []
    s_kvd, s_kvf, s_lf = [], [], []
    for layer in range(DEPTH):
        lam_init = 0.8 - 0.6 * math.exp(-0.3 * layer)
        lam = diff_lambda_value(diff_lambda[layer], lam_init)
        h = rms_norm(xp, g_attn_pre[layer])
        dq, dk, dv, fq, fk, fv, logf = project(h, w_in[layer], b_f[layer], pos_p)
        od, of = prompt_attention(dq, dk, dv, fq, fk, fv, logf, lam)
        a = merge_heads(od, of, g_subln[layer], lam_init, w_o[layer], xp.dtype)
        xp = xp + rms_norm(a, g_attn_post[layer])
        xp = ffn(xp, g_ffn_pre[layer], w_gate[layer], w_up[layer], w_down[layer], g_ffn_post[layer])
        r_d, r_f, r_l = cache_rows(dk, dv, fk, fv, logf, *dtypes)
        p_kvd.append(r_d)
        p_kvf.append(r_f)
        p_lf.append(r_l)
        h = rms_norm(xs, g_attn_pre[layer])
        dq, dk, dv, fq, fk, fv, logf = project(h, w_in[layer], b_f[layer], pos_s)
        od, of = sample_attention(dq, dk, dv, fq, fk, fv, logf, lam, cache_diff_kv, cache_fox_kv,
                                  cache_fox_logf, page_table, layer)
        a = merge_heads(od, of, g_subln[layer], lam_init, w_o[layer], xs.dtype)
        xs = xs + rms_norm(a, g_attn_post[layer])
        xs = ffn(xs, g_ffn_pre[layer], w_gate[layer], w_up[layer], w_down[layer], g_ffn_post[layer])
        r_d, r_f, r_l = cache_rows(dk, dv, fk, fv, logf, *dtypes)
        s_kvd.append(r_d)
        s_kvf.append(r_f)
        s_lf.append(r_l)
    y_prompt = xp[:, N_META:]
    y_sample = xs
    return (y_prompt, y_sample, jnp.stack(p_kvd), jnp.stack(p_kvf), jnp.stack(p_lf),
            jnp.stack(s_kvd), jnp.stack(s_kvf), jnp.stack(s_lf))
```

```python
import functools
import math

import jax
import jax.numpy as jnp
import numpy as np
from jax import lax
from jax.experimental import pallas as pl
from jax.experimental.pallas import tpu as pltpu

F32 = jnp.float32
BF16 = jnp.bfloat16

HEAD_DIM = 128
N_META = 16
PAGE_SIZE = 128
ROPE_THETA = 10000.0
EPS = 1e-6
LANES = 128
SUBLANES = 8
N_ROWS = 8
HEAD_COLS = 1024
VMEM_LIMIT = 52 * 1024 * 1024


def _cparams(n_axes, vmem=VMEM_LIMIT):
    return pltpu.CompilerParams(dimension_semantics=("arbitrary",) * n_axes,
                                vmem_limit_bytes=vmem)


def _rms(x, g):
    return x * lax.rsqrt(jnp.mean(x * x, axis=-1, keepdims=True) + EPS) * g


def _dot(a, b):
    return jnp.dot(a, b, preferred_element_type=F32)


def _dot_nt(a, b):
    return lax.dot_general(a, b, (((1,), (1,)), ((), ())), preferred_element_type=F32)


def _split3(x):
    hi = x.astype(BF16)
    r1 = x - hi.astype(F32)
    mid = r1.astype(BF16)
    lo = (r1 - mid.astype(F32)).astype(BF16)
    return hi, mid, lo


def _dot3_lhs(x, m):
    hi, mid, lo = _split3(x)
    return _dot(hi, m) + _dot(mid, m) + _dot(lo, m)


def _dot3_rhs(m, x):
    hi, mid, lo = _split3(x)
    return _dot(m, hi) + _dot(m, mid) + _dot(m, lo)


def _rope(z, cos, sin):
    parts = []
    for gi in range(z.shape[1] // HEAD_DIM):
        zg = z[:, gi * HEAD_DIM:(gi + 1) * HEAD_DIM]
        parts.append(zg * cos + pltpu.roll(zg, HEAD_DIM // 2, 1) * sin)
    return jnp.concatenate(parts, axis=1)


def _proj_kernel(x_ref, g_ref, w_ref, wfl_ref, bf_ref, cos_ref, sin_ref,
                 qkv_ref, kvd_ref, kvf_ref, logf_ref, h_scr, *, scale):
    j = pl.program_id(1)

    @pl.when(j == 0)
    def _():
        hb = _rms(x_ref[...], g_ref[...]).astype(BF16)
        h_scr[...] = hb
        zl = _dot(hb, wfl_ref[...]) + bf_ref[...]
        logf_ref[...] = jnp.minimum(zl, 0.0) - jnp.log1p(jnp.exp(-jnp.abs(zl)))

    z = _dot(h_scr[...], w_ref[...])

    @pl.when(j == 0)
    def _():
        qkv_ref[...] = (_rope(z, cos_ref[...], sin_ref[...]) * scale).astype(BF16)

    @pl.when(j == 1)
    def _():
        zr = _rope(z, cos_ref[...], sin_ref[...])
        qkv_ref[...] = zr.astype(BF16)
        kvd_ref[...] = zr

    @pl.when(j == 2)
    def _():
        qkv_ref[...] = z.astype(BF16)
        kvd_ref[...] = z

    @pl.when(j == 3)
    def _():
        qkv_ref[...] = (z * scale).astype(BF16)

    @pl.when(j == 4)
    def _():
        qkv_ref[...] = z.astype(BF16)
        kvf_ref[...] = z

    @pl.when(j == 5)
    def _():
        qkv_ref[...] = z.astype(BF16)
        kvf_ref[...] = z


def _proj(x, g, w_main, w_fl, b_fl, cos_t, sin_t, tm):
    n, d = x.shape
    n_col = w_main.shape[1] // HEAD_COLS
    tab_blocks = cos_t.shape[0] // tm
    scale = HEAD_DIM ** -0.5
    return pl.pallas_call(
        functools.partial(_proj_kernel, scale=scale),
        grid=(n // tm, n_col),
        in_specs=[
            pl.BlockSpec((tm, d), lambda i, j: (i, 0)),
            pl.BlockSpec((1, d), lambda i, j: (0, 0)),
            pl.BlockSpec((d, HEAD_COLS), lambda i, j: (0, j)),
            pl.BlockSpec((d, LANES), lambda i, j: (0, 0)),
            pl.BlockSpec((1, LANES), lambda i, j: (0, 0)),
            pl.BlockSpec((tm, HEAD_DIM), lambda i, j: (i % tab_blocks, 0)),
            pl.BlockSpec((tm, HEAD_DIM), lambda i, j: (i % tab_blocks, 0)),
        ],
        out_specs=[
            pl.BlockSpec((tm, HEAD_COLS), lambda i, j: (i, j)),
            pl.BlockSpec((tm, HEAD_COLS), lambda i, j: (i, jnp.clip(j - 1, 0, 1))),
            pl.BlockSpec((tm, HEAD_COLS), lambda i, j: (i, jnp.clip(j - 4, 0, 1))),
            pl.BlockSpec((tm, LANES), lambda i, j: (i, 0)),
        ],
        out_shape=[
            jax.ShapeDtypeStruct((n, n_col * HEAD_COLS), BF16),
            jax.ShapeDtypeStruct((n, 2 * HEAD_COLS), F32),
            jax.ShapeDtypeStruct((n, 2 * HEAD_COLS), F32),
            jax.ShapeDtypeStruct((n, LANES), F32),
        ],
        scratch_shapes=[pltpu.VMEM((tm, d), BF16)],
        compiler_params=_cparams(2),
        name="proj",
    )(x, g, w_main, w_fl, b_fl, cos_t, sin_t)


def _tri(n, strict=False):
    r = lax.broadcasted_iota(jnp.int32, (n, n), 0)
    c = lax.broadcasted_iota(jnp.int32, (n, n), 1)
    return jnp.where((c < r) if strict else (c <= r), 1.0, 0.0).astype(BF16)


def _cumsum_kernel(lf_ref, lfm_ref, out_ref, outm_ref, carry_scr, *, tc):
    c = pl.program_id(1)
    dm = _dot3_rhs(_tri(tc), lfm_ref[...])

    @pl.when(jnp.logical_and(pl.program_id(0) == 0, c == 0))
    def _():
        outm_ref[...] = dm

    @pl.when(c == 0)
    def _():
        carry_scr[...] = dm[N_META - 1:N_META, :]

    dc = _dot3_rhs(_tri(tc), lf_ref[...]) + carry_scr[...]
    out_ref[...] = dc
    carry_scr[...] = dc[tc - 1:tc, :]


def _cumsum(lf_p, lf_m, batch, tc):
    n = lf_p.shape[0]
    per = n // batch // tc
    return pl.pallas_call(
        functools.partial(_cumsum_kernel, tc=tc),
        grid=(batch, per),
        in_specs=[
            pl.BlockSpec((tc, LANES), lambda b, c: (b * per + c, 0)),
            pl.BlockSpec((tc, LANES), lambda b, c: (0, 0)),
        ],
        out_specs=[
            pl.BlockSpec((tc, LANES), lambda b, c: (b * per + c, 0)),
            pl.BlockSpec((tc, LANES), lambda b, c: (0, 0)),
        ],
        out_shape=[
            jax.ShapeDtypeStruct((n, LANES), F32),
            jax.ShapeDtypeStruct((tc, LANES), F32),
        ],
        scratch_shapes=[pltpu.VMEM((1, LANES), F32)],
        compiler_params=_cparams(2),
        name="cumsum",
    )(lf_p, jnp.pad(lf_m, ((0, tc - N_META), (0, 0))))


def _softmax_init(s, v):
    m = jnp.max(s, axis=1, keepdims=True)
    p = jnp.exp(s - m)
    return m, jnp.sum(p, axis=1, keepdims=True), _dot(p.astype(BF16), v)


def _softmax_update(state, s, v):
    m, l, acc = state
    m_new = jnp.maximum(m, jnp.max(s, axis=1, keepdims=True))
    a = jnp.exp(m - m_new)
    p = jnp.exp(s - m_new)
    return (m_new, l * a + jnp.sum(p, axis=1, keepdims=True),
            acc * a + _dot(p.astype(BF16), v))


def _attend(q, k_of, v_of, bias_of, km, vm, bias_m, qi, tq, tk):
    col_m = lax.broadcasted_iota(jnp.int32, (tq, LANES), 1)
    s = _dot_nt(q, km)
    if bias_m is not None:
        s = s + bias_m
    state = _softmax_init(jnp.where(col_m < N_META, s, -jnp.inf), vm)

    def body(j, st):
        s = _dot_nt(q, k_of(j))
        b = bias_of(j)
        if b is not None:
            s = s + b
        return _softmax_update(st, s, v_of(j))

    state = lax.fori_loop(0, qi * (tq // tk), body, state)
    row = lax.broadcasted_iota(jnp.int32, (tq, tk), 0)
    col = lax.broadcasted_iota(jnp.int32, (tq, tk), 1)
    for dj in range(tq // tk):
        j = qi * (tq // tk) + dj
        s = _dot_nt(q, k_of(j))
        b = bias_of(j)
        if b is not None:
            s = s + b
        s = jnp.where(col + dj * tk <= row, s, -jnp.inf)
        state = _softmax_update(state, s, v_of(j))
    return state


def _fox_attn_kernel(q_ref, k_ref, v_ref, km_ref, vm_ref, cq_ref, ck_ref, ckm_ref, o_ref,
                     *, tq, tk, n_heads):
    qi = pl.program_id(1)
    for h in range(n_heads):
        sl = slice(h * HEAD_DIM, (h + 1) * HEAD_DIM)
        cq = cq_ref[:, h:h + 1]

        def rows(j):
            return pl.ds(pl.multiple_of(j * tk, tk), tk)

        _, l, acc = _attend(
            q_ref[:, sl],
            lambda j: k_ref[rows(j), sl],
            lambda j: v_ref[rows(j), sl],
            lambda j: cq - ck_ref[0, h, pl.ds(j, 1), :],
            km_ref[:, sl], vm_ref[:, sl], cq - ckm_ref[h:h + 1, :],
            qi, tq, tk)
        o_ref[:, sl] = (acc / l).astype(BF16)


def _diff_attn_kernel(q_ref, k_ref, v_ref, km_ref, vm_ref, lam_ref, g_ref, o_ref,
                      *, tq, tk, n_heads, lam_init):
    qi = pl.program_id(1)
    lp = lam_ref[...]
    lam = (jnp.exp(jnp.sum(lp[0:1] * lp[1:2], axis=1, keepdims=True))
           - jnp.exp(jnp.sum(lp[2:3] * lp[3:4], axis=1, keepdims=True)) + lam_init)
    for h in range(n_heads):
        vsl = slice(h * 2 * HEAD_DIM, (h + 1) * 2 * HEAD_DIM)

        def rows(j):
            return pl.ds(pl.multiple_of(j * tk, tk), tk)

        outs = []
        for c in range(2):
            sl = slice((2 * h + c) * HEAD_DIM, (2 * h + c + 1) * HEAD_DIM)
            _, l, acc = _attend(
                q_ref[:, sl],
                lambda j, sl=sl: k_ref[rows(j), sl],
                lambda j: v_ref[rows(j), vsl],
                lambda j: None,
                km_ref[:, sl], vm_ref[:, vsl], None,
                qi, tq, tk)
            outs.append(acc / l)
        od = outs[0] - lam * outs[1]
        o_ref[:, vsl] = (_rms(od, g_ref[...]) * (1.0 - lam_init)).astype(BF16)


def _prompt_attention(qkv_p, qkv_m, cq, ck, ckm, lam_params, g_subln, batch, seq, tq, tk, lam_init):
    n = qkv_p.shape[0]
    nq = seq // tq
    pad_m = qkv_m.shape[0]

    def qmap(col):
        return lambda b, qi: (b * nq + qi, col)

    def kmap(col):
        return lambda b, qi: (b, col)

    def mmap(col):
        return lambda b, qi: (0, col)

    fox = pl.pallas_call(
        functools.partial(_fox_attn_kernel, tq=tq, tk=tk, n_heads=HEAD_COLS // HEAD_DIM),
        grid=(batch, nq),
        in_specs=[
            pl.BlockSpec((tq, HEAD_COLS), qmap(3)),
            pl.BlockSpec((seq, HEAD_COLS), kmap(4)),
            pl.BlockSpec((seq, HEAD_COLS), kmap(5)),
            pl.BlockSpec((pad_m, HEAD_COLS), mmap(4)),
            pl.BlockSpec((pad_m, HEAD_COLS), mmap(5)),
            pl.BlockSpec((tq, LANES), qmap(0)),
            pl.BlockSpec((1, N_ROWS, seq // tk, tk), lambda b, qi: (b, 0, 0, 0)),
            pl.BlockSpec((N_ROWS, LANES), lambda b, qi: (0, 0)),
        ],
        out_specs=pl.BlockSpec((tq, HEAD_COLS), qmap(0)),
        out_shape=jax.ShapeDtypeStruct((n, HEAD_COLS), BF16),
        compiler_params=_cparams(2),
        name="attn_fox",
    )(qkv_p, qkv_p, qkv_p, qkv_m, qkv_m, cq, ck, ckm)

    diff = pl.pallas_call(
        functools.partial(_diff_attn_kernel, tq=tq, tk=tk,
                          n_heads=HEAD_COLS // (2 * HEAD_DIM), lam_init=lam_init),
        grid=(batch, nq),
        in_specs=[
            pl.BlockSpec((tq, HEAD_COLS), qmap(0)),
            pl.BlockSpec((seq, HEAD_COLS), kmap(1)),
            pl.BlockSpec((seq, HEAD_COLS), kmap(2)),
            pl.BlockSpec((pad_m, HEAD_COLS), mmap(1)),
            pl.BlockSpec((pad_m, HEAD_COLS), mmap(2)),
            pl.BlockSpec((4, HEAD_DIM), lambda b, qi: (0, 0)),
            pl.BlockSpec((1, 2 * HEAD_DIM), lambda b, qi: (0, 0)),
        ],
        out_specs=pl.BlockSpec((tq, HEAD_COLS), qmap(0)),
        out_shape=jax.ShapeDtypeStruct((n, HEAD_COLS), BF16),
        compiler_params=_cparams(2),
        name="attn_diff",
    )(qkv_p, qkv_p, qkv_p, qkv_m, qkv_m, lam_params, g_subln)
    return diff, fox


FLAT = PAGE_SIZE * N_ROWS


def _sample_bias_kernel(lf_ref, ct_ref, out_ref, t_scr, p_scr):
    @pl.when(pl.program_id(0) == 0)
    def _():
        r = lax.broadcasted_iota(jnp.int32, (FLAT, FLAT), 0)
        c = lax.broadcasted_iota(jnp.int32, (FLAT, FLAT), 1)
        same = (r % N_ROWS) == (c % N_ROWS)
        p_scr[...] = jnp.where(same, 1.0, 0.0).astype(BF16)
        t_scr[...] = jnp.where(jnp.logical_and(same, r // N_ROWS > c // N_ROWS), 1.0, 0.0).astype(BF16)

    x = lf_ref[0]
    n_pages = x.shape[0]
    hi, mid, lo = _split3(x)
    within = _dot(hi, t_scr[...]) + _dot(mid, t_scr[...]) + _dot(lo, t_scr[...])
    totals = _dot(hi, p_scr[...]) + _dot(mid, p_scr[...]) + _dot(lo, p_scr[...])
    pr = lax.broadcasted_iota(jnp.int32, (n_pages, n_pages), 0)
    pc = lax.broadcasted_iota(jnp.int32, (n_pages, n_pages), 1)
    later = jnp.where(pc > pr, 1.0, 0.0).astype(BF16)
    out_ref[0] = within + _dot3_rhs(later, totals) + ct_ref[0]


def _sample_bias(lf_flat, ct_flat):
    db, n_pages, _ = lf_flat.shape
    return pl.pallas_call(
        _sample_bias_kernel,
        grid=(db,),
        in_specs=[
            pl.BlockSpec((1, n_pages, FLAT), lambda b: (b, 0, 0)),
            pl.BlockSpec((1, 1, FLAT), lambda b: (b, 0, 0)),
        ],
        out_specs=pl.BlockSpec((1, n_pages, FLAT), lambda b: (b, 0, 0)),
        out_shape=jax.ShapeDtypeStruct((db, n_pages, FLAT), F32),
        scratch_shapes=[pltpu.VMEM((FLAT, FLAT), BF16), pltpu.VMEM((FLAT, FLAT), BF16)],
        compiler_params=_cparams(1),
        name="sample_bias",
    )(lf_flat, ct_flat)


HALF = FLAT // 2


def _row_mask(width):
    lane = lax.broadcasted_iota(jnp.int32, (N_ROWS, width), 1)
    sub = lax.broadcasted_iota(jnp.int32, (N_ROWS, width), 0)
    return (lane % N_ROWS) == sub


def _col_to_periodic(col):
    return jnp.sum(jnp.where(_row_mask(LANES), jnp.broadcast_to(col, (N_ROWS, LANES)), 0.0),
                   axis=0, keepdims=True)


def _periodic_to_col(per):
    lane = lax.broadcasted_iota(jnp.int32, (N_ROWS, LANES), 1)
    sub = lax.broadcasted_iota(jnp.int32, (N_ROWS, LANES), 0)
    return jnp.sum(jnp.where(lane == sub, jnp.broadcast_to(per, (N_ROWS, LANES)), 0.0),
                   axis=1, keepdims=True)


def _periodic_reduce(x, op):
    t = x[:, 0:LANES]
    for c in range(1, x.shape[1] // LANES):
        t = op(t, x[:, c * LANES:(c + 1) * LANES])
    red = jnp.max if op is jnp.maximum else jnp.sum
    t = jnp.broadcast_to(red(t, axis=0, keepdims=True), (SUBLANES, LANES))
    sh = N_ROWS
    while sh < LANES:
        t = op(t, pltpu.roll(t, sh, 1))
        sh *= 2
    return t[0:1, :]


def _tile_lanes(per, width):
    return jnp.concatenate([per] * (width // LANES), axis=1)


def _page_rows(page_ref, kv):
    return page_ref[0, 0, :, kv, :, :].reshape(FLAT, HEAD_DIM).astype(BF16)


def _pair_lanes(rows):
    return jnp.concatenate([rows[:HALF], rows[HALF:]], axis=1)


def _flat_scores(qq, page_refs, s_scr):
    mask = _row_mask(HALF)
    for i, pref in enumerate(page_refs):
        out = _dot_nt(qq, _pair_lanes(_page_rows(pref, 0)))
        s_scr[i:i + 1, 0:HALF] = jnp.sum(jnp.where(mask, out[0:N_ROWS], 0.0), axis=0, keepdims=True)
        s_scr[i:i + 1, HALF:FLAT] = jnp.sum(jnp.where(mask, out[N_ROWS:], 0.0), axis=0, keepdims=True)


def _online_flat(sc, m_scr, l_scr):
    m_old = m_scr[...]
    m_new = jnp.maximum(m_old, _periodic_reduce(sc, jnp.maximum))
    a = jnp.exp(m_old - m_new)
    p = jnp.exp(sc - _tile_lanes(m_new, FLAT))
    l_scr[...] = l_scr[...] * a + _periodic_reduce(p, jnp.add)
    m_scr[...] = m_new
    return p, _periodic_to_col(a)


def _masked_lhs(p_row):
    pm = jnp.where(_row_mask(FLAT), jnp.broadcast_to(p_row, (N_ROWS, FLAT)), 0.0)
    return pm[:, :HALF], pm[:, HALF:]


def _swap_halves_of_8(x):
    parts = []
    for c in range(x.shape[1] // LANES):
        ch = x[:, c * LANES:(c + 1) * LANES]
        lane = lax.broadcasted_iota(jnp.int32, ch.shape, 1)
        parts.append(jnp.where((lane & 4) == 0, pltpu.roll(ch, LANES - 4, 1), pltpu.roll(ch, 4, 1)))
    return jnp.concatenate(parts, axis=1)


def _sample_attn_kernel(*refs, n_group, lam_init):
    pt_ref = refs[0]
    (qqd_ref, qqf_ref, q8d_ref, q8f_ref, knd_ref, knf_ref, vnd_ref, vnf_ref,
     lam_ref, g_ref, bias_ref) = refs[1:12]
    d_pages = refs[12:12 + n_group]
    f_pages = refs[12 + n_group:12 + 2 * n_group]
    o_ref = refs[12 + 2 * n_group]
    sd_scr, sf_scr, md_scr, ld_scr, mf_scr, lf_scr, accd_scr, accf_scr = refs[13 + 2 * n_group:]
    del pt_ref
    g = pl.program_id(1)

    @pl.when(g == 0)
    def _():
        sd = jnp.sum(q8d_ref[0].astype(F32) * knd_ref[0].astype(F32), axis=1, keepdims=True)
        sf = jnp.sum(q8f_ref[0].astype(F32) * knf_ref[0].astype(F32), axis=1, keepdims=True)
        md_scr[...] = _col_to_periodic(sd)
        mf_scr[...] = _col_to_periodic(sf)
        ld_scr[...] = jnp.ones_like(ld_scr)
        lf_scr[...] = jnp.ones_like(lf_scr)
        vd = vnd_ref[0].astype(F32)
        accd_scr[...] = jnp.concatenate([vd, vd], axis=0)
        accf_scr[...] = vnf_ref[0].astype(F32)

    _flat_scores(qqf_ref[0], f_pages, sf_scr)
    p, a_col = _online_flat(sf_scr[...] + bias_ref[0, 0], mf_scr, lf_scr)
    lhs_a, lhs_b, rhs = [], [], []
    for i in range(n_group):
        pa, pb = _masked_lhs(p[i:i + 1, :])
        lhs_a.append(pa)
        lhs_b.append(pb)
        rhs.append(_pair_lanes(_page_rows(f_pages[i], 1)))
    lhs = jnp.concatenate([jnp.concatenate(lhs_a, axis=1), jnp.concatenate(lhs_b, axis=1)], axis=0)
    o = _dot(lhs.astype(BF16), jnp.concatenate(rhs, axis=0))
    accf_scr[...] = accf_scr[...] * a_col + o[0:N_ROWS, 0:HEAD_DIM] + o[N_ROWS:, HEAD_DIM:]

    _flat_scores(qqd_ref[0], d_pages, sd_scr)
    p, a_col = _online_flat(sd_scr[...], md_scr, ld_scr)
    p_sw = _swap_halves_of_8(p)
    blocks = [[], [], [], []]
    rhs = []
    for i in range(n_group):
        pa, pb = _masked_lhs(p[i:i + 1, :])
        qa, qb = _masked_lhs(p_sw[i:i + 1, :])
        for lst, blk in zip(blocks, (pa, qa, pb, qb)):
            lst.append(blk)
        rhs.append(_pair_lanes(_page_rows(d_pages[i], 1)))
    lhs = jnp.concatenate([jnp.concatenate(b, axis=1) for b in blocks], axis=0)
    o = _dot(lhs.astype(BF16), jnp.concatenate(rhs, axis=0))
    same = o[0:8, 0:HEAD_DIM] + o[16:24, HEAD_DIM:]
    other = o[8:16, 0:HEAD_DIM] + o[24:32, HEAD_DIM:]
    a_sw = pltpu.roll(jnp.broadcast_to(a_col, (N_ROWS, LANES)), N_ROWS // 2, 0)
    accd_scr[0:N_ROWS] = accd_scr[0:N_ROWS] * a_col + same
    accd_scr[N_ROWS:] = accd_scr[N_ROWS:] * a_sw + other

    @pl.when(g == pl.num_programs(1) - 1)
    def _():
        lp = lam_ref[...]
        lam = (jnp.exp(jnp.sum(lp[0:1] * lp[1:2], axis=1, keepdims=True))
               - jnp.exp(jnp.sum(lp[2:3] * lp[3:4], axis=1, keepdims=True)) + lam_init)
        l_col = _periodic_to_col(ld_scr[...])
        l_sw = pltpu.roll(jnp.broadcast_to(l_col, (N_ROWS, LANES)), N_ROWS // 2, 0)
        o_same = accd_scr[0:N_ROWS] / l_col
        o_other = accd_scr[N_ROWS:] / l_sw
        row = lax.broadcasted_iota(jnp.int32, (N_ROWS, HEAD_DIM), 0)
        od = jnp.where(row < N_ROWS // 2, o_same - lam * o_other, o_other - lam * o_same)
        ss = jnp.sum(od * od, axis=1, keepdims=True)
        ss = jnp.broadcast_to(ss, (N_ROWS, LANES))
        ms = (ss + pltpu.roll(ss, N_ROWS // 2, 0)) * (1.0 / (2 * HEAD_DIM))
        o_ref[0, 0:N_ROWS] = od * lax.rsqrt(ms + EPS) * g_ref[...] * (1.0 - lam_init)
        o_ref[0, N_ROWS:] = accf_scr[...] / _periodic_to_col(lf_scr[...])


def _sample_attention(page_table, cache_d, cache_f, bias, qq_d, qq_f, q8_d, q8_f,
                      kn_d, kn_f, vn_d, vn_f, lam_params, g_rows, n_group, lam_init):
    db, n_pages = page_table.shape
    steps = n_pages // n_group

    def seq3(shape):
        return pl.BlockSpec((1,) + shape, lambda b, g, pt: (b, 0, 0))

    def page_spec(i):
        return pl.BlockSpec((1, 1, PAGE_SIZE, 2, N_ROWS, HEAD_DIM),
                            lambda b, g, pt: (0, pt[b, g * n_group + i], 0, 0, 0, 0))

    in_specs = [
        seq3((2 * N_ROWS, 2 * HEAD_DIM)), seq3((2 * N_ROWS, 2 * HEAD_DIM)),
        seq3((N_ROWS, HEAD_DIM)), seq3((N_ROWS, HEAD_DIM)),
        seq3((N_ROWS, HEAD_DIM)), seq3((N_ROWS, HEAD_DIM)),
        seq3((N_ROWS, HEAD_DIM)), seq3((N_ROWS, HEAD_DIM)),
        pl.BlockSpec((4, HEAD_DIM), lambda b, g, pt: (0, 0)),
        pl.BlockSpec((N_ROWS, HEAD_DIM), lambda b, g, pt: (0, 0)),
        pl.BlockSpec((1, 1, n_group, FLAT), lambda b, g, pt: (b, g, 0, 0)),
    ]
    in_specs += [page_spec(i) for i in range(n_group)] * 2
    return pl.pallas_call(
        functools.partial(_sample_attn_kernel, n_group=n_group, lam_init=lam_init),
        grid_spec=pltpu.PrefetchScalarGridSpec(
            num_scalar_prefetch=1,
            grid=(db, steps),
            in_specs=in_specs,
            out_specs=pl.BlockSpec((1, 2 * N_ROWS, HEAD_DIM), lambda b, g, pt: (b, 0, 0)),
            scratch_shapes=[
                pltpu.VMEM((n_group, FLAT), F32), pltpu.VMEM((n_group, FLAT), F32),
                pltpu.VMEM((1, LANES), F32), pltpu.VMEM((1, LANES), F32),
                pltpu.VMEM((1, LANES), F32), pltpu.VMEM((1, LANES), F32),
                pltpu.VMEM((2 * N_ROWS, HEAD_DIM), F32), pltpu.VMEM((N_ROWS, HEAD_DIM), F32),
            ],
        ),
        out_shape=jax.ShapeDtypeStruct((db, 2 * N_ROWS, HEAD_DIM), F32),
        compiler_params=_cparams(2),
        name="sample_attn",
    )(page_table, qq_d, qq_f, q8_d, q8_f, kn_d, kn_f, vn_d, vn_f, lam_params, g_rows,
      bias.reshape(db, steps, n_group, FLAT), *([cache_d] * n_group), *([cache_f] * n_group))


def _merge_kernel(od_ref, of_ref, x_ref, wd_ref, wf_ref, g_ref, o_ref):
    a = _dot(od_ref[...], wd_ref[...]) + _dot(of_ref[...], wf_ref[...])
    o_ref[...] = x_ref[...] + _rms(a, g_ref[...])


def _merge(od, of, x, w_od, w_of, g, tm):
    n, d = x.shape
    half = od.shape[1]
    return pl.pallas_call(
        _merge_kernel,
        grid=(n // tm,),
        in_specs=[
            pl.BlockSpec((tm, half), lambda i: (i, 0)),
            pl.BlockSpec((tm, half), lambda i: (i, 0)),
            pl.BlockSpec((tm, d), lambda i: (i, 0)),
            pl.BlockSpec((half, d), lambda i: (0, 0)),
            pl.BlockSpec((half, d), lambda i: (0, 0)),
            pl.BlockSpec((1, d), lambda i: (0, 0)),
        ],
        out_specs=pl.BlockSpec((tm, d), lambda i: (i, 0)),
        out_shape=jax.ShapeDtypeStruct((n, d), F32),
        compiler_params=_cparams(1),
        name="merge",
    )(od, of, x, w_od, w_of, g)


def _ffn_kernel(x_ref, gpre_ref, wg_ref, wu_ref, wd_ref, gpost_ref, o_ref, h_scr, acc_scr):
    j = pl.program_id(1)

    @pl.when(j == 0)
    def _():
        h_scr[...] = _rms(x_ref[...], gpre_ref[...]).astype(BF16)
        acc_scr[...] = jnp.zeros_like(acc_scr)

    hb = h_scr[...]
    gate = _dot(hb, wg_ref[...])
    up = _dot(hb, wu_ref[...])
    u = gate * jax.nn.sigmoid(gate) * up
    acc_scr[...] += _dot(u.astype(BF16), wd_ref[...])

    @pl.when(j == pl.num_programs(1) - 1)
    def _():
        o_ref[...] = x_ref[...] + _rms(acc_scr[...], gpost_ref[...])


def _ffn(x, g_pre, w_gate, w_up, w_down, g_post, tm, tf):
    n, d = x.shape
    f = w_gate.shape[1]
    return pl.pallas_call(
        _ffn_kernel,
        grid=(n // tm, f // tf),
        in_specs=[
            pl.BlockSpec((tm, d), lambda i, j: (i, 0)),
            pl.BlockSpec((1, d), lambda i, j: (0, 0)),
            pl.BlockSpec((d, tf), lambda i, j: (0, j)),
            pl.BlockSpec((d, tf), lambda i, j: (0, j)),
            pl.BlockSpec((tf, d), lambda i, j: (j, 0)),
            pl.BlockSpec((1, d), lambda i, j: (0, 0)),
        ],
        out_specs=pl.BlockSpec((tm, d), lambda i, j: (i, 0)),
        out_shape=jax.ShapeDtypeStruct((n, d), F32),
        scratch_shapes=[pltpu.VMEM((tm, d), BF16), pltpu.VMEM((tm, d), F32)],
        compiler_params=_cparams(2),
        name="ffn",
    )(x, g_pre, w_gate, w_up, w_down, g_post)


def _rope_tables(pos):
    half = HEAD_DIM // 2
    inv = ROPE_THETA ** (-jnp.arange(half, dtype=F32) / half)
    ang = pos.astype(F32)[:, None] * inv[None, :]
    cos, sin = jnp.cos(ang), jnp.sin(ang)
    return jnp.concatenate([cos, cos], axis=1), jnp.concatenate([-sin, sin], axis=1)


def _tile_rows(n, target):
    t = min(n, target)
    while n % t:
        t //= 2
    return t


def _as_key_tiles(cache):
    _, pool, page, _, rows, width = cache.shape
    halves = width // HEAD_DIM
    c = cache.reshape(1, pool, page, 2, rows, halves, HEAD_DIM)
    return jnp.transpose(c, (0, 1, 2, 3, 5, 4, 6)).reshape(1, pool, page, 2, rows * halves, HEAD_DIM)


def _rows_half_head(v):
    n = v.shape[0]
    return jnp.transpose(v.reshape(n, 4, 2, HEAD_DIM), (0, 2, 1, 3)).reshape(n, N_ROWS, HEAD_DIM)


def kernel(x_prompt, x_sample, cache_diff_kv, cache_fox_kv, cache_fox_logf, page_table, meta_tokens, g_attn_pre, w_in, b_f, diff_lambda, g_subln, w_o, g_attn_post, g_ffn_pre, w_gate, w_up, w_down, g_ffn_post):
    batch, seq, d_model = x_prompt.shape
    db, t_new, _ = x_sample.shape
    assert t_new == 1 and w_in.shape[0] == 1
    n_pages = page_table.shape[1]
    past = n_pages * PAGE_SIZE
    lam_init = 0.8 - 0.6 * math.exp(-0.3 * 0)
    n_p = batch * seq
    n_main = 6 * HEAD_COLS

    w_main = w_in[0][:, :n_main].astype(BF16)
    w_fl = jnp.pad(w_in[0][:, n_main:], ((0, 0), (0, LANES - N_ROWS))).astype(BF16)
    b_fl = jnp.pad(b_f[0].astype(F32), (0, LANES - N_ROWS))[None, :]
    g_pre = g_attn_pre[0][None, :].astype(F32)

    tm_p = _tile_rows(seq, 512)
    cos_p, sin_p = _rope_tables(N_META + jnp.arange(seq))
    xp = x_prompt.reshape(n_p, d_model)
    qkv_p, kvd_p, kvf_p, lf_p = _proj(xp, g_pre, w_main, w_fl, b_fl, cos_p, sin_p, tm_p)

    n_e = N_META + db
    x_e = jnp.concatenate([meta_tokens.astype(F32), x_sample.reshape(db, d_model)], axis=0)
    pos_e = jnp.concatenate([jnp.arange(N_META), jnp.full((db,), past)])
    cos_e, sin_e = _rope_tables(pos_e)
    qkv_e, kvd_e, kvf_e, lf_e = _proj(x_e, g_pre, w_main, w_fl, b_fl, cos_e, sin_e, n_e)

    tc = _tile_rows(seq, 256)
    dcum_p, dcum_m = _cumsum(lf_p, lf_e[:N_META], batch, tc)
    tq = _tile_rows(seq, 512)
    tk = tq
    ck = jnp.transpose(dcum_p[:, :N_ROWS].reshape(batch, seq, N_ROWS), (0, 2, 1))
    ck = ck.reshape(batch, N_ROWS, seq // tk, tk)
    ckm = jnp.pad(dcum_m[:N_META, :N_ROWS].T, ((0, 0), (0, LANES - N_META)))
    qkv_m = jnp.pad(qkv_e[:N_META], ((0, LANES - N_META), (0, 0)))
    lam_params = diff_lambda[0].astype(F32)
    g_sub = g_subln[0].astype(F32)[None, :]
    od_p, of_p = _prompt_attention(qkv_p, qkv_m, dcum_p, ck, ckm, lam_params, g_sub,
                                   batch, seq, tq, tk, lam_init)

    qkv_s = qkv_e[N_META:]
    lf_s = lf_e[N_META:, :N_ROWS]
    cols = lambda a, j: a[:, j * HEAD_COLS:(j + 1) * HEAD_COLS]
    q8_d = _rows_half_head(cols(qkv_s, 0))
    kn_d = _rows_half_head(cols(qkv_s, 1))
    vn_d = _rows_half_head(cols(qkv_s, 2))
    q8_f = cols(qkv_s, 3).reshape(db, N_ROWS, HEAD_DIM)
    kn_f = cols(qkv_s, 4).reshape(db, N_ROWS, HEAD_DIM)
    vn_f = cols(qkv_s, 5).reshape(db, N_ROWS, HEAD_DIM)

    def paired(q8):
        z = jnp.zeros_like(q8)
        return jnp.concatenate([jnp.concatenate([q8, z], axis=2), jnp.concatenate([z, q8], axis=2)], axis=1)

    lf_flat = cache_fox_logf[0][page_table].astype(F32).reshape(db, n_pages, FLAT)
    ct_flat = jnp.tile(lf_s, (1, PAGE_SIZE))[:, None, :]
    bias = _sample_bias(lf_flat, ct_flat)
    g_rows = jnp.repeat(g_subln[0].astype(F32).reshape(2, 1, HEAD_DIM), 4, axis=1).reshape(N_ROWS, HEAD_DIM)
    n_group = 4 if n_pages % 4 == 0 else 1
    o_s = _sample_attention(page_table, _as_key_tiles(cache_diff_kv), _as_key_tiles(cache_fox_kv), bias,
                            paired(q8_d), paired(q8_f), q8_d, q8_f, kn_d, kn_f, vn_d, vn_f,
                            lam_params, g_rows, n_group, lam_init)
    od_s = jnp.transpose(o_s[:, :N_ROWS].reshape(db, 2, 4, HEAD_DIM), (0, 2, 1, 3)).reshape(db, HEAD_COLS)
    of_s = o_s[:, N_ROWS:].reshape(db, HEAD_COLS)

    w_od = w_o[0][:HEAD_COLS].astype(BF16)
    w_of = w_o[0][HEAD_COLS:].astype(BF16)
    g_post = g_attn_post[0][None, :].astype(F32)
    g_fpre = g_ffn_pre[0][None, :].astype(F32)
    g_fpost = g_ffn_post[0][None, :].astype(F32)
    wg, wu, wd = w_gate[0].astype(BF16), w_up[0].astype(BF16), w_down[0].astype(BF16)
    tf = 512
    x1_p = _merge(od_p, of_p, xp, w_od, w_of, g_post, _tile_rows(n_p, 256))
    y_p = _ffn(x1_p, g_fpre, wg, wu, wd, g_fpost, _tile_rows(n_p, 512), tf)
    xs = x_sample.reshape(db, d_model)
    x1_s = _merge(od_s.astype(BF16), of_s.astype(BF16), xs, w_od, w_of, g_post, db)
    y_s = _ffn(x1_s, g_fpre, wg, wu, wd, g_fpost, db, tf)

    def with_meta(rows_e, rows_p, tail):
        m = jnp.broadcast_to(rows_e[:N_META][None], (batch, N_META) + rows_e.shape[1:])
        full = jnp.concatenate([m, rows_p.reshape((batch, seq) + rows_p.shape[1:])], axis=1)
        return full.reshape((1, batch, N_META + seq) + tail)

    dkv_p = with_meta(kvd_e, kvd_p, (2, 4, 2 * HEAD_DIM)).astype(cache_diff_kv.dtype)
    fkv_p = with_meta(kvf_e, kvf_p, (2, N_ROWS, HEAD_DIM)).astype(cache_fox_kv.dtype)
    flf_p = with_meta(lf_e[:, :N_ROWS], lf_p[:, :N_ROWS], (N_ROWS,)).astype(cache_fox_logf.dtype)
    dkv_s = kvd_e[N_META:].reshape(1, db, 1, 2, 4, 2 * HEAD_DIM).astype(cache_diff_kv.dtype)
    fkv_s = kvf_e[N_META:].reshape(1, db, 1, 2, N_ROWS, HEAD_DIM).astype(cache_fox_kv.dtype)
    flf_s = lf_s.reshape(1, db, 1, N_ROWS).astype(cache_fox_logf.dtype)
    return (y_p.reshape(batch, seq, d_model), y_s.reshape(db, 1, d_model),
            dkv_p, fkv_p, flf_p, dkv_s, fkv_s, flf_s)
```

```python
import functools
import math

import jax
import jax.numpy as jnp
from jax import lax
from jax.experimental import pallas as pl
from jax.experimental.pallas import tpu as pltpu

F32 = jnp.float32
BF16 = jnp.bfloat16

HEAD_DIM = 128
N_META = 16
PAGE_SIZE = 128
ROPE_THETA = 10000.0
EPS = 1e-6
LANES = 128
SUBLANES = 8
N_ROWS = 8
HEAD_COLS = 1024
VMEM_LIMIT = 52 * 1024 * 1024
FUSED_VMEM_LIMIT = 58 * 1024 * 1024
LOG2E = math.log2(math.e)


def _cparams(n_axes, vmem=VMEM_LIMIT):
    return pltpu.CompilerParams(dimension_semantics=("arbitrary",) * n_axes,
                                vmem_limit_bytes=vmem)


def _rms(x, g):
    return x * lax.rsqrt(jnp.mean(x * x, axis=-1, keepdims=True) + EPS) * g


def _dot(a, b):
    return jnp.dot(a, b, preferred_element_type=F32)


def _dot_nt(a, b):
    return lax.dot_general(a, b, (((1,), (1,)), ((), ())), preferred_element_type=F32)


def _split3(x):
    hi = x.astype(BF16)
    r1 = x - hi.astype(F32)
    mid = r1.astype(BF16)
    lo = (r1 - mid.astype(F32)).astype(BF16)
    return hi, mid, lo


def _dot3_rhs(m, x):
    hi, mid, lo = _split3(x)
    return _dot(m, hi) + _dot(m, mid) + _dot(m, lo)


def _rope(z, cos, sin):
    parts = []
    for gi in range(z.shape[1] // HEAD_DIM):
        zg = z[:, gi * HEAD_DIM:(gi + 1) * HEAD_DIM]
        parts.append(zg * cos + pltpu.roll(zg, HEAD_DIM // 2, 1) * sin)
    return jnp.concatenate(parts, axis=1)


def _proj_kernel(x_ref, g_ref, w_ref, wfl_ref, bf_ref, cos_ref, sin_ref,
                 qkv_ref, kvd_ref, kvf_ref, logf_ref, h_scr, *, scale):
    j = pl.program_id(1)

    @pl.when(j == 0)
    def _():
        hb = _rms(x_ref[...], g_ref[...]).astype(BF16)
        h_scr[...] = hb
        zl = _dot(hb, wfl_ref[...]) + bf_ref[...]
        logf_ref[...] = jnp.minimum(zl, 0.0) - jnp.log1p(jnp.exp(-jnp.abs(zl)))

    z = _dot(h_scr[...], w_ref[...])

    @pl.when(j == 0)
    def _():
        qkv_ref[...] = (_rope(z, cos_ref[...], sin_ref[...]) * scale).astype(BF16)

    @pl.when(j == 1)
    def _():
        zr = _rope(z, cos_ref[...], sin_ref[...])
        qkv_ref[...] = zr.astype(BF16)
        kvd_ref[...] = zr

    @pl.when(j == 2)
    def _():
        qkv_ref[...] = z.astype(BF16)
        kvd_ref[...] = z

    @pl.when(j == 3)
    def _():
        qkv_ref[...] = (z * scale).astype(BF16)

    @pl.when(j == 4)
    def _():
        qkv_ref[...] = z.astype(BF16)
        kvf_ref[...] = z

    @pl.when(j == 5)
    def _():
        qkv_ref[...] = z.astype(BF16)
        kvf_ref[...] = z


def _proj(x, g, w_main, w_fl, b_fl, cos_t, sin_t, tm, scale):
    n, d = x.shape
    n_col = w_main.shape[1] // HEAD_COLS
    tab_blocks = cos_t.shape[0] // tm
    return pl.pallas_call(
        functools.partial(_proj_kernel, scale=scale),
        grid=(n // tm, n_col),
        in_specs=[
            pl.BlockSpec((tm, d), lambda i, j: (i, 0)),
            pl.BlockSpec((1, d), lambda i, j: (0, 0)),
            pl.BlockSpec((d, HEAD_COLS), lambda i, j: (0, j)),
            pl.BlockSpec((d, LANES), lambda i, j: (0, 0)),
            pl.BlockSpec((1, LANES), lambda i, j: (0, 0)),
            pl.BlockSpec((tm, HEAD_DIM), lambda i, j: (i % tab_blocks, 0)),
            pl.BlockSpec((tm, HEAD_DIM), lambda i, j: (i % tab_blocks, 0)),
        ],
        out_specs=[
            pl.BlockSpec((tm, HEAD_COLS), lambda i, j: (i, j)),
            pl.BlockSpec((tm, HEAD_COLS), lambda i, j: (i, jnp.clip(j - 1, 0, 1))),
            pl.BlockSpec((tm, HEAD_COLS), lambda i, j: (i, jnp.clip(j - 4, 0, 1))),
            pl.BlockSpec((tm, LANES), lambda i, j: (i, 0)),
        ],
        out_shape=[
            jax.ShapeDtypeStruct((n, n_col * HEAD_COLS), BF16),
            jax.ShapeDtypeStruct((n, 2 * HEAD_COLS), F32),
            jax.ShapeDtypeStruct((n, 2 * HEAD_COLS), F32),
            jax.ShapeDtypeStruct((n, LANES), F32),
        ],
        scratch_shapes=[pltpu.VMEM((tm, d), BF16)],
        compiler_params=_cparams(2),
        name="proj",
    )(x, g, w_main, w_fl, b_fl, cos_t, sin_t)


def _tri(n):
    r = lax.broadcasted_iota(jnp.int32, (n, n), 0)
    c = lax.broadcasted_iota(jnp.int32, (n, n), 1)
    return jnp.where(c <= r, 1.0, 0.0).astype(BF16)


def _cumsum_kernel(lf_ref, lfm_ref, out_ref, outm_ref, *, tc, scale):
    tri = _tri(tc)
    dm = _dot3_rhs(tri, lfm_ref[...])

    @pl.when(pl.program_id(0) == 0)
    def _():
        outm_ref[...] = dm * scale

    carry = dm[N_META - 1:N_META, :]
    for c in range(lf_ref.shape[0] // tc):
        dc = _dot3_rhs(tri, lf_ref[c * tc:(c + 1) * tc, :]) + carry
        out_ref[c * tc:(c + 1) * tc, :] = dc * scale
        carry = dc[tc - 1:tc, :]


def _cumsum(lf_p, lf_m, batch, tc, scale):
    n = lf_p.shape[0]
    seq = n // batch
    return pl.pallas_call(
        functools.partial(_cumsum_kernel, tc=tc, scale=scale),
        grid=(batch,),
        in_specs=[
            pl.BlockSpec((seq, LANES), lambda b: (b, 0)),
            pl.BlockSpec((tc, LANES), lambda b: (0, 0)),
        ],
        out_specs=[
            pl.BlockSpec((seq, LANES), lambda b: (b, 0)),
            pl.BlockSpec((tc, LANES), lambda b: (0, 0)),
        ],
        out_shape=[
            jax.ShapeDtypeStruct((n, LANES), F32),
            jax.ShapeDtypeStruct((tc, LANES), F32),
        ],
        compiler_params=_cparams(1),
        name="cumsum",
    )(lf_p, jnp.pad(lf_m, ((0, tc - N_META), (0, 0))))


def _softmax_init(s, v):
    m = jnp.max(s, axis=1, keepdims=True)
    p = jnp.exp2(s - m)
    return m, jnp.sum(p, axis=1, keepdims=True), _dot(p.astype(BF16), v)


def _softmax_update(state, s, v):
    m, l, acc = state
    m_new = jnp.maximum(m, jnp.max(s, axis=1, keepdims=True))
    a = jnp.exp2(m - m_new)
    p = jnp.exp2(s - m_new)
    return (m_new, l * a + jnp.sum(p, axis=1, keepdims=True),
            acc * a + _dot(p.astype(BF16), v))


def _attend(q, k_of, v_of, bias_of, km, vm, bias_m, qi, tq, tk):
    col_m = lax.broadcasted_iota(jnp.int32, (tq, LANES), 1)
    s = _dot_nt(q, km)
    if bias_m is not None:
        s = s + bias_m
    state = _softmax_init(jnp.where(col_m < N_META, s, -jnp.inf), vm)

    def body(j, st):
        s = _dot_nt(q, k_of(j))
        b = bias_of(j)
        if b is not None:
            s = s + b
        return _softmax_update(st, s, v_of(j))

    state = lax.fori_loop(0, qi * (tq // tk), body, state)
    row = lax.broadcasted_iota(jnp.int32, (tq, tk), 0)
    col = lax.broadcasted_iota(jnp.int32, (tq, tk), 1)
    for dj in range(tq // tk):
        j = qi * (tq // tk) + dj
        s = _dot_nt(q, k_of(j))
        b = bias_of(j)
        if b is not None:
            s = s + b
        s = jnp.where(col + dj * tk <= row, s, -jnp.inf)
        state = _softmax_update(state, s, v_of(j))
    return state


def _fox_attn_kernel(q_ref, k_ref, v_ref, km_ref, vm_ref, cq_ref, ck_ref, ckm_ref, o_ref,
                     *, tq, tk, n_heads):
    qi = pl.program_id(1)
    for h in range(n_heads):
        sl = slice(h * HEAD_DIM, (h + 1) * HEAD_DIM)
        cq = cq_ref[:, h:h + 1]

        def rows(j):
            return pl.ds(pl.multiple_of(j * tk, tk), tk)

        _, l, acc = _attend(
            q_ref[:, sl],
            lambda j: k_ref[rows(j), sl],
            lambda j: v_ref[rows(j), sl],
            lambda j: cq - ck_ref[0, h, pl.ds(j, 1), :],
            km_ref[:, sl], vm_ref[:, sl], cq - ckm_ref[h:h + 1, :],
            qi, tq, tk)
        o_ref[:, sl] = (acc / l).astype(BF16)


def _diff_attn_kernel(q_ref, k_ref, v_ref, km_ref, vm_ref, lam_ref, g_ref, o_ref,
                      *, tq, tk, n_heads, lam_init):
    qi = pl.program_id(1)
    lp = lam_ref[...]
    lam = (jnp.exp(jnp.sum(lp[0:1] * lp[1:2], axis=1, keepdims=True))
           - jnp.exp(jnp.sum(lp[2:3] * lp[3:4], axis=1, keepdims=True)) + lam_init)
    for h in range(n_heads):
        vsl = slice(h * 2 * HEAD_DIM, (h + 1) * 2 * HEAD_DIM)

        def rows(j):
            return pl.ds(pl.multiple_of(j * tk, tk), tk)

        outs = []
        for c in range(2):
            sl = slice((2 * h + c) * HEAD_DIM, (2 * h + c + 1) * HEAD_DIM)
            _, l, acc = _attend(
                q_ref[:, sl],
                lambda j, sl=sl: k_ref[rows(j), sl],
                lambda j: v_ref[rows(j), vsl],
                lambda j: None,
                km_ref[:, sl], vm_ref[:, vsl], None,
                qi, tq, tk)
            outs.append(acc / l)
        od = outs[0] - lam * outs[1]
        o_ref[:, vsl] = (_rms(od, g_ref[...]) * (1.0 - lam_init)).astype(BF16)


def _prompt_attention(qkv_p, qkv_m, cq, ck, ckm, lam_params, g_subln, batch, seq, tq, tk, lam_init):
    n = qkv_p.shape[0]
    nq = seq // tq
    pad_m = qkv_m.shape[0]

    def qmap(col):
        return lambda b, qi: (b * nq + qi, col)

    def kmap(col):
        return lambda b, qi: (b, col)

    def mmap(col):
        return lambda b, qi: (0, col)

    fox = pl.pallas_call(
        functools.partial(_fox_attn_kernel, tq=tq, tk=tk, n_heads=HEAD_COLS // HEAD_DIM),
        grid=(batch, nq),
        in_specs=[
            pl.BlockSpec((tq, HEAD_COLS), qmap(3)),
            pl.BlockSpec((seq, HEAD_COLS), kmap(4)),
            pl.BlockSpec((seq, HEAD_COLS), kmap(5)),
            pl.BlockSpec((pad_m, HEAD_COLS), mmap(4)),
            pl.BlockSpec((pad_m, HEAD_COLS), mmap(5)),
            pl.BlockSpec((tq, LANES), qmap(0)),
            pl.BlockSpec((1, N_ROWS, seq // tk, tk), lambda b, qi: (b, 0, 0, 0)),
            pl.BlockSpec((N_ROWS, LANES), lambda b, qi: (0, 0)),
        ],
        out_specs=pl.BlockSpec((tq, HEAD_COLS), qmap(0)),
        out_shape=jax.ShapeDtypeStruct((n, HEAD_COLS), BF16),
        compiler_params=_cparams(2),
        name="attn_fox",
    )(qkv_p, qkv_p, qkv_p, qkv_m, qkv_m, cq, ck, ckm)

    diff = pl.pallas_call(
        functools.partial(_diff_attn_kernel, tq=tq, tk=tk,
                          n_heads=HEAD_COLS // (2 * HEAD_DIM), lam_init=lam_init),
        grid=(batch, nq),
        in_specs=[
            pl.BlockSpec((tq, HEAD_COLS), qmap(0)),
            pl.BlockSpec((seq, HEAD_COLS), kmap(1)),
            pl.BlockSpec((seq, HEAD_COLS), kmap(2)),
            pl.BlockSpec((pad_m, HEAD_COLS), mmap(1)),
            pl.BlockSpec((pad_m, HEAD_COLS), mmap(2)),
            pl.BlockSpec((4, HEAD_DIM), lambda b, qi: (0, 0)),
            pl.BlockSpec((1, 2 * HEAD_DIM), lambda b, qi: (0, 0)),
        ],
        out_specs=pl.BlockSpec((tq, HEAD_COLS), qmap(0)),
        out_shape=jax.ShapeDtypeStruct((n, HEAD_COLS), BF16),
        compiler_params=_cparams(2),
        name="attn_diff",
    )(qkv_p, qkv_p, qkv_p, qkv_m, qkv_m, lam_params, g_subln)
    return diff, fox


FLAT = PAGE_SIZE * N_ROWS


def _sample_bias_kernel(lf_ref, ct_ref, out_ref, t_scr, p_scr):
    @pl.when(pl.program_id(0) == 0)
    def _():
        r = lax.broadcasted_iota(jnp.int32, (FLAT, FLAT), 0)
        c = lax.broadcasted_iota(jnp.int32, (FLAT, FLAT), 1)
        same = (r % N_ROWS) == (c % N_ROWS)
        p_scr[...] = jnp.where(same, 1.0, 0.0).astype(BF16)
        t_scr[...] = jnp.where(jnp.logical_and(same, r // N_ROWS > c // N_ROWS), 1.0, 0.0).astype(BF16)

    x = lf_ref[0]
    n_pages = x.shape[0]
    hi, mid, lo = _split3(x)
    within = _dot(hi, t_scr[...]) + _dot(mid, t_scr[...]) + _dot(lo, t_scr[...])
    totals = _dot(hi, p_scr[...]) + _dot(mid, p_scr[...]) + _dot(lo, p_scr[...])
    pr = lax.broadcasted_iota(jnp.int32, (n_pages, n_pages), 0)
    pc = lax.broadcasted_iota(jnp.int32, (n_pages, n_pages), 1)
    later = jnp.where(pc > pr, 1.0, 0.0).astype(BF16)
    out_ref[0] = within + _dot3_rhs(later, totals) + ct_ref[0]


def _sample_bias(lf_flat, ct_flat):
    db, n_pages, _ = lf_flat.shape
    return pl.pallas_call(
        _sample_bias_kernel,
        grid=(db,),
        in_specs=[
            pl.BlockSpec((1, n_pages, FLAT), lambda b: (b, 0, 0)),
            pl.BlockSpec((1, 1, FLAT), lambda b: (b, 0, 0)),
        ],
        out_specs=pl.BlockSpec((1, n_pages, FLAT), lambda b: (b, 0, 0)),
        out_shape=jax.ShapeDtypeStruct((db, n_pages, FLAT), F32),
        scratch_shapes=[pltpu.VMEM((FLAT, FLAT), BF16), pltpu.VMEM((FLAT, FLAT), BF16)],
        compiler_params=_cparams(1),
        name="sample_bias",
    )(lf_flat, ct_flat)


HALF = FLAT // 2


def _row_mask(width):
    lane = lax.broadcasted_iota(jnp.int32, (N_ROWS, width), 1)
    sub = lax.broadcasted_iota(jnp.int32, (N_ROWS, width), 0)
    return (lane % N_ROWS) == sub


def _col_to_periodic(col):
    return jnp.sum(jnp.where(_row_mask(LANES), jnp.broadcast_to(col, (N_ROWS, LANES)), 0.0),
                   axis=0, keepdims=True)


def _periodic_to_col(per):
    lane = lax.broadcasted_iota(jnp.int32, (N_ROWS, LANES), 1)
    sub = lax.broadcasted_iota(jnp.int32, (N_ROWS, LANES), 0)
    return jnp.sum(jnp.where(lane == sub, jnp.broadcast_to(per, (N_ROWS, LANES)), 0.0),
                   axis=1, keepdims=True)


def _periodic_reduce(x, op):
    t = x[:, 0:LANES]
    for c in range(1, x.shape[1] // LANES):
        t = op(t, x[:, c * LANES:(c + 1) * LANES])
    red = jnp.max if op is jnp.maximum else jnp.sum
    t = jnp.broadcast_to(red(t, axis=0, keepdims=True), (SUBLANES, LANES))
    sh = N_ROWS
    while sh < LANES:
        t = op(t, pltpu.roll(t, sh, 1))
        sh *= 2
    return t[0:1, :]


def _tile_lanes(per, width):
    return jnp.concatenate([per] * (width // LANES), axis=1)


def _page_rows(page_ref, kv):
    return page_ref[0, 0, :, kv, :, :].reshape(FLAT, HEAD_DIM).astype(BF16)


def _pair_lanes(rows):
    return jnp.concatenate([rows[:HALF], rows[HALF:]], axis=1)


def _flat_scores(qq, page_refs, s_scr):
    mask = _row_mask(HALF)
    for i, pref in enumerate(page_refs):
        out = _dot_nt(qq, _pair_lanes(_page_rows(pref, 0)))
        s_scr[i:i + 1, 0:HALF] = jnp.sum(jnp.where(mask, out[0:N_ROWS], 0.0), axis=0, keepdims=True)
        s_scr[i:i + 1, HALF:FLAT] = jnp.sum(jnp.where(mask, out[N_ROWS:], 0.0), axis=0, keepdims=True)


def _online_flat(sc, m_old, l_old):
    m_new = jnp.maximum(m_old, _periodic_reduce(sc, jnp.maximum))
    a = jnp.exp(m_old - m_new)
    p = jnp.exp(sc - _tile_lanes(m_new, FLAT))
    return p, m_new, l_old * a + _periodic_reduce(p, jnp.add), _periodic_to_col(a)


def _masked_lhs(p_row):
    pm = jnp.where(_row_mask(FLAT), jnp.broadcast_to(p_row, (N_ROWS, FLAT)), 0.0)
    return pm[:, :HALF], pm[:, HALF:]


def _swap_halves_of_8(x):
    parts = []
    for c in range(x.shape[1] // LANES):
        ch = x[:, c * LANES:(c + 1) * LANES]
        lane = lax.broadcasted_iota(jnp.int32, ch.shape, 1)
        parts.append(jnp.where((lane & 4) == 0, pltpu.roll(ch, LANES - 4, 1), pltpu.roll(ch, 4, 1)))
    return jnp.concatenate(parts, axis=1)


def _decode_init(q8d_ref, q8f_ref, knd_ref, knf_ref, vnd_ref, vnf_ref):
    sd = jnp.sum(q8d_ref[0].astype(F32) * knd_ref[0].astype(F32), axis=1, keepdims=True)
    sf = jnp.sum(q8f_ref[0].astype(F32) * knf_ref[0].astype(F32), axis=1, keepdims=True)
    ones = jnp.ones((1, LANES), F32)
    vd = vnd_ref[0].astype(F32)
    return (_col_to_periodic(sd), ones, jnp.concatenate([vd, vd], axis=0),
            _col_to_periodic(sf), ones, vnf_ref[0].astype(F32))


def _decode_update(state, qqd, qqf, bias, d_pages, f_pages, sd_scr, sf_scr):
    md, ld, accd, mf, lf, accf = state
    n_group = len(d_pages)

    _flat_scores(qqf, f_pages, sf_scr)
    p, mf, lf, a_col = _online_flat(sf_scr[...] + bias, mf, lf)
    lhs_a, lhs_b, rhs = [], [], []
    for i in range(n_group):
        pa, pb = _masked_lhs(p[i:i + 1, :])
        lhs_a.append(pa)
        lhs_b.append(pb)
        rhs.append(_pair_lanes(_page_rows(f_pages[i], 1)))
    lhs = jnp.concatenate([jnp.concatenate(lhs_a, axis=1), jnp.concatenate(lhs_b, axis=1)], axis=0)
    o = _dot(lhs.astype(BF16), jnp.concatenate(rhs, axis=0))
    accf = accf * a_col + o[0:N_ROWS, 0:HEAD_DIM] + o[N_ROWS:, HEAD_DIM:]

    _flat_scores(qqd, d_pages, sd_scr)
    p, md, ld, a_col = _online_flat(sd_scr[...], md, ld)
    p_sw = _swap_halves_of_8(p)
    blocks = [[], [], [], []]
    rhs = []
    for i in range(n_group):
        pa, pb = _masked_lhs(p[i:i + 1, :])
        qa, qb = _masked_lhs(p_sw[i:i + 1, :])
        for lst, blk in zip(blocks, (pa, qa, pb, qb)):
            lst.append(blk)
        rhs.append(_pair_lanes(_page_rows(d_pages[i], 1)))
    lhs = jnp.concatenate([jnp.concatenate(b, axis=1) for b in blocks], axis=0)
    o = _dot(lhs.astype(BF16), jnp.concatenate(rhs, axis=0))
    same = o[0:8, 0:HEAD_DIM] + o[16:24, HEAD_DIM:]
    other = o[8:16, 0:HEAD_DIM] + o[24:32, HEAD_DIM:]
    a_sw = pltpu.roll(jnp.broadcast_to(a_col, (N_ROWS, LANES)), N_ROWS // 2, 0)
    accd = jnp.concatenate([accd[0:N_ROWS] * a_col + same, accd[N_ROWS:] * a_sw + other], axis=0)
    return md, ld, accd, mf, lf, accf


def _decode_output(state, lam_ref, g_ref, lam_init):
    md, ld, accd, mf, lf, accf = state
    lp = lam_ref[...]
    lam = (jnp.exp(jnp.sum(lp[0:1] * lp[1:2], axis=1, keepdims=True))
           - jnp.exp(jnp.sum(lp[2:3] * lp[3:4], axis=1, keepdims=True)) + lam_init)
    l_col = _periodic_to_col(ld)
    l_sw = pltpu.roll(jnp.broadcast_to(l_col, (N_ROWS, LANES)), N_ROWS // 2, 0)
    o_same = accd[0:N_ROWS] / l_col
    o_other = accd[N_ROWS:] / l_sw
    row = lax.broadcasted_iota(jnp.int32, (N_ROWS, HEAD_DIM), 0)
    od = jnp.where(row < N_ROWS // 2, o_same - lam * o_other, o_other - lam * o_same)
    ss = jnp.broadcast_to(jnp.sum(od * od, axis=1, keepdims=True), (N_ROWS, LANES))
    ms = (ss + pltpu.roll(ss, N_ROWS // 2, 0)) * (1.0 / (2 * HEAD_DIM))
    od = od * lax.rsqrt(ms + EPS) * g_ref[...] * (1.0 - lam_init)
    return jnp.concatenate([od, accf / _periodic_to_col(lf)], axis=0)


def _merge_kernel(od_ref, of_ref, x_ref, wd_ref, wf_ref, g_ref, o_ref):
    a = _dot(od_ref[...], wd_ref[...]) + _dot(of_ref[...], wf_ref[...])
    o_ref[...] = x_ref[...] + _rms(a, g_ref[...])


def _merge(od, of, x, w_od, w_of, g, tm):
    n, d = x.shape
    half = od.shape[1]
    return pl.pallas_call(
        _merge_kernel,
        grid=(n // tm,),
        in_specs=[
            pl.BlockSpec((tm, half), lambda i: (i, 0)),
            pl.BlockSpec((tm, half), lambda i: (i, 0)),
            pl.BlockSpec((tm, d), lambda i: (i, 0)),
            pl.BlockSpec((half, d), lambda i: (0, 0)),
            pl.BlockSpec((half, d), lambda i: (0, 0)),
            pl.BlockSpec((1, d), lambda i: (0, 0)),
        ],
        out_specs=pl.BlockSpec((tm, d), lambda i: (i, 0)),
        out_shape=jax.ShapeDtypeStruct((n, d), F32),
        compiler_params=_cparams(1),
        name="merge",
    )(od, of, x, w_od, w_of, g)


def _ffn_prologue(x_ref, gpre_ref, h_scr, acc_scr):
    @pl.when(pl.program_id(1) == 0)
    def _():
        h_scr[...] = _rms(x_ref[...], gpre_ref[...]).astype(BF16)
        acc_scr[...] = jnp.zeros_like(acc_scr)


def _ffn_main(wg_ref, wu_ref, wd_ref, h_scr, acc_scr):
    hb = h_scr[...]
    gate = _dot(hb, wg_ref[...])
    up = _dot(hb, wu_ref[...])
    u = gate * jax.nn.sigmoid(gate) * up
    acc_scr[...] += _dot(u.astype(BF16), wd_ref[...])


def _ffn_epilogue(x_ref, gpost_ref, o_ref, acc_scr):
    @pl.when(pl.program_id(1) == pl.num_programs(1) - 1)
    def _():
        o_ref[...] = x_ref[...] + _rms(acc_scr[...], gpost_ref[...])


def _ffn_kernel(x_ref, gpre_ref, wg_ref, wu_ref, wd_ref, gpost_ref, o_ref, h_scr, acc_scr):
    _ffn_prologue(x_ref, gpre_ref, h_scr, acc_scr)
    _ffn_main(wg_ref, wu_ref, wd_ref, h_scr, acc_scr)
    _ffn_epilogue(x_ref, gpost_ref, o_ref, acc_scr)


def _ffn_specs(tm, d, tf, imap):
    return [
        pl.BlockSpec((tm, d), imap(lambda i, j: (i, 0))),
        pl.BlockSpec((1, d), imap(lambda i, j: (0, 0))),
        pl.BlockSpec((d, tf), imap(lambda i, j: (0, j))),
        pl.BlockSpec((d, tf), imap(lambda i, j: (0, j))),
        pl.BlockSpec((tf, d), imap(lambda i, j: (j, 0))),
        pl.BlockSpec((1, d), imap(lambda i, j: (0, 0))),
    ]


def _ffn(x, g_pre, w_gate, w_up, w_down, g_post, tm, tf):
    n, d = x.shape
    f = w_gate.shape[1]
    return pl.pallas_call(
        _ffn_kernel,
        grid=(n // tm, f // tf),
        in_specs=_ffn_specs(tm, d, tf, lambda fn: fn),
        out_specs=pl.BlockSpec((tm, d), lambda i, j: (i, 0)),
        out_shape=jax.ShapeDtypeStruct((n, d), F32),
        scratch_shapes=[pltpu.VMEM((tm, d), BF16), pltpu.VMEM((tm, d), F32)],
        compiler_params=_cparams(2),
        name="ffn",
    )(x, g_pre, w_gate, w_up, w_down, g_post)


def _ffn_decode_kernel(*refs, n_group, steps_per_seq, n_decode_steps, lam_init):
    pt_ref = refs[0]
    x_ref, gpre_ref, wg_ref, wu_ref, wd_ref, gpost_ref = refs[1:7]
    (qqd_ref, qqf_ref, q8d_ref, q8f_ref, knd_ref, knf_ref, vnd_ref, vnf_ref,
     lam_ref, g_ref, bias_ref) = refs[7:18]
    d_pages = refs[18:18 + n_group]
    f_pages = refs[18 + n_group:18 + 2 * n_group]
    y_ref, os_ref = refs[18 + 2 * n_group:20 + 2 * n_group]
    h_scr, acc_scr, sd_scr, sf_scr = refs[20 + 2 * n_group:24 + 2 * n_group]
    state_scr = refs[24 + 2 * n_group:]
    del pt_ref
    t = pl.program_id(0) * pl.num_programs(1) + pl.program_id(1)
    _ffn_prologue(x_ref, gpre_ref, h_scr, acc_scr)

    @pl.when(t == 0)
    def _():
        for scr in state_scr:
            scr[...] = jnp.zeros_like(scr)

    @pl.when(t < n_decode_steps)
    def _():
        _ffn_main(wg_ref, wu_ref, wd_ref, h_scr, acc_scr)
        first = (t % steps_per_seq) == 0
        cur = tuple(scr[...] for scr in state_scr)
        init = _decode_init(q8d_ref, q8f_ref, knd_ref, knf_ref, vnd_ref, vnf_ref)
        start = tuple(jnp.where(first, a, b) for a, b in zip(init, cur))
        new = _decode_update(start, qqd_ref[0], qqf_ref[0], bias_ref[0, 0], d_pages, f_pages,
                             sd_scr, sf_scr)
        for scr, val in zip(state_scr, new):
            scr[...] = val
        os_ref[0] = _decode_output(new, lam_ref, g_ref, lam_init)

    @pl.when(t >= n_decode_steps)
    def _():
        _ffn_main(wg_ref, wu_ref, wd_ref, h_scr, acc_scr)

    _ffn_epilogue(x_ref, gpost_ref, y_ref, acc_scr)


def _ffn_decode(x, g_pre, w_gate, w_up, w_down, g_post, tm, tf,
                page_table, cache_d, cache_f, bias, qq_d, qq_f, q8_d, q8_f,
                kn_d, kn_f, vn_d, vn_f, lam_params, g_rows, n_group, lam_init):
    n, d = x.shape
    f = w_gate.shape[1]
    nj = f // tf
    db, n_pages = page_table.shape
    sps = n_pages // n_group
    n_dec = db * sps
    assert (n // tm) * nj >= n_dec

    def seq_of(i, j):
        t = jnp.minimum(i * nj + j, n_dec - 1)
        return t // sps, t % sps

    def seq3(shape):
        return pl.BlockSpec((1,) + shape, lambda i, j, pt: (seq_of(i, j)[0], 0, 0))

    def page_spec(k):
        def imap(i, j, pt):
            b, g = seq_of(i, j)
            return (0, pt[b, g * n_group + k], 0, 0, 0, 0)
        return pl.BlockSpec((1, 1, PAGE_SIZE, 2, N_ROWS, HEAD_DIM), imap)

    in_specs = _ffn_specs(tm, d, tf, lambda fn: (lambda i, j, pt: fn(i, j)))
    in_specs += [
        seq3((2 * N_ROWS, 2 * HEAD_DIM)), seq3((2 * N_ROWS, 2 * HEAD_DIM)),
        seq3((N_ROWS, HEAD_DIM)), seq3((N_ROWS, HEAD_DIM)),
        seq3((N_ROWS, HEAD_DIM)), seq3((N_ROWS, HEAD_DIM)),
        seq3((N_ROWS, HEAD_DIM)), seq3((N_ROWS, HEAD_DIM)),
        pl.BlockSpec((4, HEAD_DIM), lambda i, j, pt: (0, 0)),
        pl.BlockSpec((N_ROWS, HEAD_DIM), lambda i, j, pt: (0, 0)),
        pl.BlockSpec((1, 1, n_group, FLAT), lambda i, j, pt: seq_of(i, j) + (0, 0)),
    ]
    in_specs += [page_spec(k) for k in range(n_group)] * 2
    return pl.pallas_call(
        functools.partial(_ffn_decode_kernel, n_group=n_group, steps_per_seq=sps,
                          n_decode_steps=n_dec, lam_init=lam_init),
        grid_spec=pltpu.PrefetchScalarGridSpec(
            num_scalar_prefetch=1,
            grid=(n // tm, nj),
            in_specs=in_specs,
            out_specs=[
                pl.BlockSpec((tm, d), lambda i, j, pt: (i, 0)),
                pl.BlockSpec((1, 2 * N_ROWS, HEAD_DIM), lambda i, j, pt: (seq_of(i, j)[0], 0, 0)),
            ],
            scratch_shapes=[
                pltpu.VMEM((tm, d), BF16), pltpu.VMEM((tm, d), F32),
                pltpu.VMEM((n_group, FLAT), F32), pltpu.VMEM((n_group, FLAT), F32),
                pltpu.VMEM((1, LANES), F32), pltpu.VMEM((1, LANES), F32),
                pltpu.VMEM((2 * N_ROWS, HEAD_DIM), F32),
                pltpu.VMEM((1, LANES), F32), pltpu.VMEM((1, LANES), F32),
                pltpu.VMEM((N_ROWS, HEAD_DIM), F32),
            ],
        ),
        out_shape=[
            jax.ShapeDtypeStruct((n, d), F32),
            jax.ShapeDtypeStruct((db, 2 * N_ROWS, HEAD_DIM), F32),
        ],
        compiler_params=_cparams(2, FUSED_VMEM_LIMIT),
        name="ffn_decode",
    )(page_table, x, g_pre, w_gate, w_up, w_down, g_post,
      qq_d, qq_f, q8_d, q8_f, kn_d, kn_f, vn_d, vn_f, lam_params, g_rows,
      bias.reshape(db, sps, n_group, FLAT), *([cache_d] * n_group), *([cache_f] * n_group))


def _rope_tables(pos):
    half = HEAD_DIM // 2
    inv = ROPE_THETA ** (-jnp.arange(half, dtype=F32) / half)
    ang = pos.astype(F32)[:, None] * inv[None, :]
    cos, sin = jnp.cos(ang), jnp.sin(ang)
    return jnp.concatenate([cos, cos], axis=1), jnp.concatenate([-sin, sin], axis=1)


def _tile_rows(n, target):
    t = min(n, target)
    while n % t:
        t //= 2
    return t


def _as_key_tiles(cache):
    _, pool, page, _, rows, width = cache.shape
    halves = width // HEAD_DIM
    c = cache.reshape(1, pool, page, 2, rows, halves, HEAD_DIM)
    return jnp.transpose(c, (0, 1, 2, 3, 5, 4, 6)).reshape(1, pool, page, 2, rows * halves, HEAD_DIM)


def _rows_half_head(v):
    n = v.shape[0]
    return jnp.transpose(v.reshape(n, 4, 2, HEAD_DIM), (0, 2, 1, 3)).reshape(n, N_ROWS, HEAD_DIM)


def kernel(x_prompt, x_sample, cache_diff_kv, cache_fox_kv, cache_fox_logf, page_table, meta_tokens, g_attn_pre, w_in, b_f, diff_lambda, g_subln, w_o, g_attn_post, g_ffn_pre, w_gate, w_up, w_down, g_ffn_post):
    batch, seq, d_model = x_prompt.shape
    db, t_new, _ = x_sample.shape
    assert t_new == 1 and w_in.shape[0] == 1
    n_pages = page_table.shape[1]
    past = n_pages * PAGE_SIZE
    lam_init = 0.8 - 0.6 * math.exp(-0.3 * 0)
    n_p = batch * seq
    n_main = 6 * HEAD_COLS

    w_main = w_in[0][:, :n_main].astype(BF16)
    w_fl = jnp.pad(w_in[0][:, n_main:], ((0, 0), (0, LANES - N_ROWS))).astype(BF16)
    b_fl = jnp.pad(b_f[0].astype(F32), (0, LANES - N_ROWS))[None, :]
    g_pre = g_attn_pre[0][None, :].astype(F32)

    tm_p = _tile_rows(seq, 512)
    cos_p, sin_p = _rope_tables(N_META + jnp.arange(seq))
    xp = x_prompt.reshape(n_p, d_model)
    scale = HEAD_DIM ** -0.5
    qkv_p, kvd_p, kvf_p, lf_p = _proj(xp, g_pre, w_main, w_fl, b_fl, cos_p, sin_p, tm_p, scale * LOG2E)

    n_e = N_META + db
    x_e = jnp.concatenate([meta_tokens.astype(F32), x_sample.reshape(db, d_model)], axis=0)
    pos_e = jnp.concatenate([jnp.arange(N_META), jnp.full((db,), past)])
    cos_e, sin_e = _rope_tables(pos_e)
    qkv_e, kvd_e, kvf_e, lf_e = _proj(x_e, g_pre, w_main, w_fl, b_fl, cos_e, sin_e, n_e, scale)

    tc = _tile_rows(seq, 256)
    dcum_p, dcum_m = _cumsum(lf_p, lf_e[:N_META], batch, tc, LOG2E)
    tq = _tile_rows(seq, 512)
    tk = tq
    ck = jnp.transpose(dcum_p[:, :N_ROWS].reshape(batch, seq, N_ROWS), (0, 2, 1))
    ck = ck.reshape(batch, N_ROWS, seq // tk, tk)
    ckm = jnp.pad(dcum_m[:N_META, :N_ROWS].T, ((0, 0), (0, LANES - N_META)))
    qkv_m = jnp.pad(qkv_e[:N_META], ((0, LANES - N_META), (0, 0)))
    lam_params = diff_lambda[0].astype(F32)
    g_sub = g_subln[0].astype(F32)[None, :]
    od_p, of_p = _prompt_attention(qkv_p, qkv_m, dcum_p, ck, ckm, lam_params, g_sub,
                                   batch, seq, tq, tk, lam_init)

    qkv_s = qkv_e[N_META:]
    lf_s = lf_e[N_META:, :N_ROWS]
    cols = lambda a, j: a[:, j * HEAD_COLS:(j + 1) * HEAD_COLS]
    q8_d = _rows_half_head(cols(qkv_s, 0))
    kn_d = _rows_half_head(cols(qkv_s, 1))
    vn_d = _rows_half_head(cols(qkv_s, 2))
    q8_f = cols(qkv_s, 3).reshape(db, N_ROWS, HEAD_DIM)
    kn_f = cols(qkv_s, 4).reshape(db, N_ROWS, HEAD_DIM)
    vn_f = cols(qkv_s, 5).reshape(db, N_ROWS, HEAD_DIM)

    def paired(q8):
        z = jnp.zeros_like(q8)
        return jnp.concatenate([jnp.concatenate([q8, z], axis=2), jnp.concatenate([z, q8], axis=2)], axis=1)

    lf_flat = cache_fox_logf[0][page_table].astype(F32).reshape(db, n_pages, FLAT)
    ct_flat = jnp.tile(lf_s, (1, PAGE_SIZE))[:, None, :]
    bias = _sample_bias(lf_flat, ct_flat)
    g_rows = jnp.repeat(g_subln[0].astype(F32).reshape(2, 1, HEAD_DIM), 4, axis=1).reshape(N_ROWS, HEAD_DIM)
    n_group = 4 if n_pages % 4 == 0 else 1

    w_od = w_o[0][:HEAD_COLS].astype(BF16)
    w_of = w_o[0][HEAD_COLS:].astype(BF16)
    g_post = g_attn_post[0][None, :].astype(F32)
    g_fpre = g_ffn_pre[0][None, :].astype(F32)
    g_fpost = g_ffn_post[0][None, :].astype(F32)
    wg, wu, wd = w_gate[0].astype(BF16), w_up[0].astype(BF16), w_down[0].astype(BF16)
    x1_p = _merge(od_p, of_p, xp, w_od, w_of, g_post, _tile_rows(n_p, 256))
    y_p, o_s = _ffn_decode(x1_p, g_fpre, wg, wu, wd, g_fpost, _tile_rows(n_p, 512), 256,
                           page_table, _as_key_tiles(cache_diff_kv), _as_key_tiles(cache_fox_kv), bias,
                           paired(q8_d), paired(q8_f), q8_d, q8_f, kn_d, kn_f, vn_d, vn_f,
                           lam_params, g_rows, n_group, lam_init)
    od_s = jnp.transpose(o_s[:, :N_ROWS].reshape(db, 2, 4, HEAD_DIM), (0, 2, 1, 3)).reshape(db, HEAD_COLS)
    of_s = o_s[:, N_ROWS:].reshape(db, HEAD_COLS)
    xs = x_sample.reshape(db, d_model)
    x1_s = _merge(od_s.astype(BF16), of_s.astype(BF16), xs, w_od, w_of, g_post, db)
    y_s = _ffn(x1_s, g_fpre, wg, wu, wd, g_fpost, db, 512)

    def with_meta(rows_e, rows_p, tail):
        m = jnp.broadcast_to(rows_e[:N_META][None], (batch, N_META) + rows_e.shape[1:])
        full = jnp.concatenate([m, rows_p.reshape((batch, seq) + rows_p.shape[1:])], axis=1)
        return full.reshape((1, batch, N_META + seq) + tail)

    dkv_p = with_meta(kvd_e, kvd_p, (2, 4, 2 * HEAD_DIM)).astype(cache_diff_kv.dtype)
    fkv_p = with_meta(kvf_e, kvf_p, (2, N_ROWS, HEAD_DIM)).astype(cache_fox_kv.dtype)
    flf_p = with_meta(lf_e[:, :N_ROWS], lf_p[:, :N_ROWS], (N_ROWS,)).astype(cache_fox_logf.dtype)
    dkv_s = kvd_e[N_META:].reshape(1, db, 1, 2, 4, 2 * HEAD_DIM).astype(cache_diff_kv.dtype)
    fkv_s = kvf_e[N_META:].reshape(1, db, 1, 2, N_ROWS, HEAD_DIM).astype(cache_fox_kv.dtype)
    flf_s = lf_s.reshape(1, db, 1, N_ROWS).astype(cache_fox_logf.dtype)
    return (y_p.reshape(batch, seq, d_model), y_s.reshape(db, 1, d_model),
            dkv_p, fkv_p, flf_p, dkv_s, fkv_s, flf_s)
```

```python
import functools
import math

import jax
import jax.numpy as jnp
from jax import lax
from jax.experimental import pallas as pl
from jax.experimental.pallas import tpu as pltpu

F32 = jnp.float32
BF16 = jnp.bfloat16

HEAD_DIM = 128
N_META = 16
PAGE_SIZE = 128
ROPE_THETA = 10000.0
EPS = 1e-6
LANES = 128
SUBLANES = 8
N_ROWS = 8
HEAD_COLS = 1024
VMEM_LIMIT = 52 * 1024 * 1024
LARGE_VMEM_LIMIT = 58 * 1024 * 1024
LOG2E = math.log2(math.e)


def _cparams(n_axes, vmem=VMEM_LIMIT):
    return pltpu.CompilerParams(dimension_semantics=("arbitrary",) * n_axes,
                                vmem_limit_bytes=vmem)


def _rms(x, g):
    return x * lax.rsqrt(jnp.mean(x * x, axis=-1, keepdims=True) + EPS) * g


def _dot(a, b):
    return jnp.dot(a, b, preferred_element_type=F32)


def _dot_nt(a, b):
    return lax.dot_general(a, b, (((1,), (1,)), ((), ())), preferred_element_type=F32)


def _split3(x):
    hi = x.astype(BF16)
    r1 = x - hi.astype(F32)
    mid = r1.astype(BF16)
    lo = (r1 - mid.astype(F32)).astype(BF16)
    return hi, mid, lo


def _dot3_rhs(m, x):
    hi, mid, lo = _split3(x)
    return _dot(m, hi) + _dot(m, mid) + _dot(m, lo)


def _proj_kernel(x_ref, g_ref, w_ref, wfl_ref, bf_ref, a_ref, b_ref,
                 qkv_ref, kv_ref, logf_ref, h_scr):
    @pl.when(pl.program_id(1) == 0)
    def _():
        hb = _rms(x_ref[...], g_ref[...]).astype(BF16)
        h_scr[...] = hb
        zl = _dot(hb, wfl_ref[...]) + bf_ref[...]
        logf_ref[...] = jnp.minimum(zl, 0.0) - jnp.log1p(jnp.exp(-jnp.abs(zl)))

    z = _dot(h_scr[...], w_ref[...])
    a, b = a_ref[0], b_ref[0]
    parts = []
    for gi in range(HEAD_COLS // HEAD_DIM):
        zg = z[:, gi * HEAD_DIM:(gi + 1) * HEAD_DIM]
        parts.append(zg * a + pltpu.roll(zg, HEAD_DIM // 2, 1) * b)
    out = jnp.concatenate(parts, axis=1)
    qkv_ref[...] = out.astype(BF16)
    kv_ref[...] = out


def _proj_wcol(j):
    return j + jnp.where(j == 2, 1, 0) - jnp.where(j == 3, 1, 0)


def _proj_kvcol(j):
    return jnp.maximum(j - 1, 0) - jnp.where(j >= 3, 1, 0)


def _proj(x, g, w_main, w_fl, b_fl, tab_a, tab_b, tm):
    n, d = x.shape
    n_col = w_main.shape[1] // HEAD_COLS
    tab_blocks = tab_a.shape[1] // tm
    tab_spec = pl.BlockSpec((1, tm, HEAD_DIM), lambda i, j: (jnp.minimum(j, 3), i % tab_blocks, 0))
    return pl.pallas_call(
        _proj_kernel,
        grid=(n // tm, n_col),
        in_specs=[
            pl.BlockSpec((tm, d), lambda i, j: (i, 0)),
            pl.BlockSpec((1, d), lambda i, j: (0, 0)),
            pl.BlockSpec((d, HEAD_COLS), lambda i, j: (0, _proj_wcol(j))),
            pl.BlockSpec((d, LANES), lambda i, j: (0, 0)),
            pl.BlockSpec((1, LANES), lambda i, j: (0, 0)),
            tab_spec, tab_spec,
        ],
        out_specs=[
            pl.BlockSpec((tm, HEAD_COLS), lambda i, j: (i, _proj_wcol(j))),
            pl.BlockSpec((tm, HEAD_COLS), lambda i, j: (i, _proj_kvcol(j))),
            pl.BlockSpec((tm, LANES), lambda i, j: (i, 0)),
        ],
        out_shape=[
            jax.ShapeDtypeStruct((n, n_col * HEAD_COLS), BF16),
            jax.ShapeDtypeStruct((n, 4 * HEAD_COLS), F32),
            jax.ShapeDtypeStruct((n, LANES), F32),
        ],
        scratch_shapes=[pltpu.VMEM((tm, d), BF16)],
        compiler_params=_cparams(2, LARGE_VMEM_LIMIT),
        name="proj",
    )(x, g, w_main, w_fl, b_fl, tab_a, tab_b)


def _tri(n):
    r = lax.broadcasted_iota(jnp.int32, (n, n), 0)
    c = lax.broadcasted_iota(jnp.int32, (n, n), 1)
    return jnp.where(c <= r, 1.0, 0.0).astype(BF16)


def _cumsum_kernel(lf_ref, lfm_ref, out_ref, outm_ref, *, tc, scale):
    tri = _tri(tc)
    dm = _dot3_rhs(tri, lfm_ref[...])

    @pl.when(pl.program_id(0) == 0)
    def _():
        outm_ref[...] = dm * scale

    carry = dm[N_META - 1:N_META, :]
    for c in range(lf_ref.shape[0] // tc):
        dc = _dot3_rhs(tri, lf_ref[c * tc:(c + 1) * tc, :]) + carry
        out_ref[c * tc:(c + 1) * tc, :] = dc * scale
        carry = dc[tc - 1:tc, :]


def _cumsum(lf_p, lf_m, batch, tc, scale):
    n = lf_p.shape[0]
    seq = n // batch
    return pl.pallas_call(
        functools.partial(_cumsum_kernel, tc=tc, scale=scale),
        grid=(batch,),
        in_specs=[
            pl.BlockSpec((seq, LANES), lambda b: (b, 0)),
            pl.BlockSpec((tc, LANES), lambda b: (0, 0)),
        ],
        out_specs=[
            pl.BlockSpec((seq, LANES), lambda b: (b, 0)),
            pl.BlockSpec((tc, LANES), lambda b: (0, 0)),
        ],
        out_shape=[
            jax.ShapeDtypeStruct((n, LANES), F32),
            jax.ShapeDtypeStruct((tc, LANES), F32),
        ],
        compiler_params=_cparams(1),
        name="cumsum",
    )(lf_p, jnp.pad(lf_m, ((0, tc - N_META), (0, 0))))


def _softmax_init(s, v):
    m = jnp.max(s, axis=1, keepdims=True)
    p = jnp.exp2(s - m)
    return m, jnp.sum(p, axis=1, keepdims=True), _dot(p.astype(BF16), v)


def _softmax_update(state, s, v):
    m, l, acc = state
    m_new = jnp.maximum(m, jnp.max(s, axis=1, keepdims=True))
    a = jnp.exp2(m - m_new)
    p = jnp.exp2(s - m_new)
    return (m_new, l * a + jnp.sum(p, axis=1, keepdims=True),
            acc * a + _dot(p.astype(BF16), v))


def _attend(q, k_of, v_of, bias_of, km, vm, bias_m, qi, tq, tk):
    col_m = lax.broadcasted_iota(jnp.int32, (tq, LANES), 1)
    s = _dot_nt(q, km)
    if bias_m is not None:
        s = s + bias_m
    state = _softmax_init(jnp.where(col_m < N_META, s, -jnp.inf), vm)

    def body(j, st):
        s = _dot_nt(q, k_of(j))
        b = bias_of(j)
        if b is not None:
            s = s + b
        return _softmax_update(st, s, v_of(j))

    state = lax.fori_loop(0, qi * (tq // tk), body, state)
    row = lax.broadcasted_iota(jnp.int32, (tq, tk), 0)
    col = lax.broadcasted_iota(jnp.int32, (tq, tk), 1)
    for dj in range(tq // tk):
        j = qi * (tq // tk) + dj
        s = _dot_nt(q, k_of(j))
        b = bias_of(j)
        if b is not None:
            s = s + b
        s = jnp.where(col + dj * tk <= row, s, -jnp.inf)
        state = _softmax_update(state, s, v_of(j))
    return state


def _fox_attn_kernel(q_ref, k_ref, v_ref, km_ref, vm_ref, cq_ref, ck_ref, ckm_ref, o_ref,
                     *, tq, tk, n_heads):
    qi = pl.program_id(1)
    for h in range(n_heads):
        sl = slice(h * HEAD_DIM, (h + 1) * HEAD_DIM)
        cq = cq_ref[:, h:h + 1]

        def rows(j):
            return pl.ds(pl.multiple_of(j * tk, tk), tk)

        _, l, acc = _attend(
            q_ref[:, sl],
            lambda j: k_ref[rows(j), sl],
            lambda j: v_ref[rows(j), sl],
            lambda j: cq - ck_ref[0, h, pl.ds(j, 1), :],
            km_ref[:, sl], vm_ref[:, sl], cq - ckm_ref[h:h + 1, :],
            qi, tq, tk)
        o_ref[:, sl] = (acc / l).astype(BF16)


def _diff_attn_kernel(q_ref, k_ref, v_ref, km_ref, vm_ref, lam_ref, g_ref, o_ref,
                      *, tq, tk, n_heads, lam_init):
    qi = pl.program_id(1)
    lp = lam_ref[...]
    lam = (jnp.exp(jnp.sum(lp[0:1] * lp[1:2], axis=1, keepdims=True))
           - jnp.exp(jnp.sum(lp[2:3] * lp[3:4], axis=1, keepdims=True)) + lam_init)
    for h in range(n_heads):
        vsl = slice(h * 2 * HEAD_DIM, (h + 1) * 2 * HEAD_DIM)

        def rows(j):
            return pl.ds(pl.multiple_of(j * tk, tk), tk)

        outs = []
        for c in range(2):
            sl = slice((2 * h + c) * HEAD_DIM, (2 * h + c + 1) * HEAD_DIM)
            _, l, acc = _attend(
                q_ref[:, sl],
                lambda j, sl=sl: k_ref[rows(j), sl],
                lambda j: v_ref[rows(j), vsl],
                lambda j: None,
                km_ref[:, sl], vm_ref[:, vsl], None,
                qi, tq, tk)
            outs.append(acc / l)
        od = outs[0] - lam * outs[1]
        o_ref[:, vsl] = (_rms(od, g_ref[...]) * (1.0 - lam_init)).astype(BF16)


def _prompt_attention(qkv_p, qkv_m, cq, ck, ckm, lam_params, g_subln, batch, seq, tq, tk, lam_init):
    n = qkv_p.shape[0]
    nq = seq // tq
    pad_m = qkv_m.shape[0]

    def qmap(col):
        return lambda b, qi: (b * nq + qi, col)

    def kmap(col):
        return lambda b, qi: (b, col)

    def mmap(col):
        return lambda b, qi: (0, col)

    fox = pl.pallas_call(
        functools.partial(_fox_attn_kernel, tq=tq, tk=tk, n_heads=HEAD_COLS // HEAD_DIM),
        grid=(batch, nq),
        in_specs=[
            pl.BlockSpec((tq, HEAD_COLS), qmap(3)),
            pl.BlockSpec((seq, HEAD_COLS), kmap(4)),
            pl.BlockSpec((seq, HEAD_COLS), kmap(5)),
            pl.BlockSpec((pad_m, HEAD_COLS), mmap(4)),
            pl.BlockSpec((pad_m, HEAD_COLS), mmap(5)),
            pl.BlockSpec((tq, LANES), qmap(0)),
            pl.BlockSpec((1, N_ROWS, seq // tk, tk), lambda b, qi: (b, 0, 0, 0)),
            pl.BlockSpec((N_ROWS, LANES), lambda b, qi: (0, 0)),
        ],
        out_specs=pl.BlockSpec((tq, HEAD_COLS), qmap(0)),
        out_shape=jax.ShapeDtypeStruct((n, HEAD_COLS), BF16),
        compiler_params=_cparams(2),
        name="attn_fox",
    )(qkv_p, qkv_p, qkv_p, qkv_m, qkv_m, cq, ck, ckm)

    diff = pl.pallas_call(
        functools.partial(_diff_attn_kernel, tq=tq, tk=tk,
                          n_heads=HEAD_COLS // (2 * HEAD_DIM), lam_init=lam_init),
        grid=(batch, nq),
        in_specs=[
            pl.BlockSpec((tq, HEAD_COLS), qmap(0)),
            pl.BlockSpec((seq, HEAD_COLS), kmap(1)),
            pl.BlockSpec((seq, HEAD_COLS), kmap(2)),
            pl.BlockSpec((pad_m, HEAD_COLS), mmap(1)),
            pl.BlockSpec((pad_m, HEAD_COLS), mmap(2)),
            pl.BlockSpec((4, HEAD_DIM), lambda b, qi: (0, 0)),
            pl.BlockSpec((1, 2 * HEAD_DIM), lambda b, qi: (0, 0)),
        ],
        out_specs=pl.BlockSpec((tq, HEAD_COLS), qmap(0)),
        out_shape=jax.ShapeDtypeStruct((n, HEAD_COLS), BF16),
        compiler_params=_cparams(2),
        name="attn_diff",
    )(qkv_p, qkv_p, qkv_p, qkv_m, qkv_m, lam_params, g_subln)
    return diff, fox


FLAT = PAGE_SIZE * N_ROWS


def _sample_bias_kernel(lf_ref, ct_ref, out_ref, t_scr, p_scr):
    @pl.when(pl.program_id(0) == 0)
    def _():
        r = lax.broadcasted_iota(jnp.int32, (FLAT, FLAT), 0)
        c = lax.broadcasted_iota(jnp.int32, (FLAT, FLAT), 1)
        same = (r % N_ROWS) == (c % N_ROWS)
        p_scr[...] = jnp.where(same, 1.0, 0.0).astype(BF16)
        t_scr[...] = jnp.where(jnp.logical_and(same, r // N_ROWS > c // N_ROWS), 1.0, 0.0).astype(BF16)

    x = lf_ref[0]
    n_pages = x.shape[0]
    hi, mid, lo = _split3(x)
    within = _dot(hi, t_scr[...]) + _dot(mid, t_scr[...]) + _dot(lo, t_scr[...])
    totals = _dot(hi, p_scr[...]) + _dot(mid, p_scr[...]) + _dot(lo, p_scr[...])
    pr = lax.broadcasted_iota(jnp.int32, (n_pages, n_pages), 0)
    pc = lax.broadcasted_iota(jnp.int32, (n_pages, n_pages), 1)
    later = jnp.where(pc > pr, 1.0, 0.0).astype(BF16)
    out_ref[0] = within + _dot3_rhs(later, totals) + ct_ref[0]


def _sample_bias(lf_flat, ct_flat):
    db, n_pages, _ = lf_flat.shape
    return pl.pallas_call(
        _sample_bias_kernel,
        grid=(db,),
        in_specs=[
            pl.BlockSpec((1, n_pages, FLAT), lambda b: (b, 0, 0)),
            pl.BlockSpec((1, 1, FLAT), lambda b: (b, 0, 0)),
        ],
        out_specs=pl.BlockSpec((1, n_pages, FLAT), lambda b: (b, 0, 0)),
        out_shape=jax.ShapeDtypeStruct((db, n_pages, FLAT), F32),
        scratch_shapes=[pltpu.VMEM((FLAT, FLAT), BF16), pltpu.VMEM((FLAT, FLAT), BF16)],
        compiler_params=_cparams(1),
        name="sample_bias",
    )(lf_flat, ct_flat)


HALF = FLAT // 2


def _row_mask(width):
    lane = lax.broadcasted_iota(jnp.int32, (N_ROWS, width), 1)
    sub = lax.broadcasted_iota(jnp.int32, (N_ROWS, width), 0)
    return (lane % N_ROWS) == sub


def _col_to_periodic(col):
    return jnp.sum(jnp.where(_row_mask(LANES), jnp.broadcast_to(col, (N_ROWS, LANES)), 0.0),
                   axis=0, keepdims=True)


def _periodic_to_col(per):
    lane = lax.broadcasted_iota(jnp.int32, (N_ROWS, LANES), 1)
    sub = lax.broadcasted_iota(jnp.int32, (N_ROWS, LANES), 0)
    return jnp.sum(jnp.where(lane == sub, jnp.broadcast_to(per, (N_ROWS, LANES)), 0.0),
                   axis=1, keepdims=True)


def _periodic_reduce(x, op):
    t = x[:, 0:LANES]
    for c in range(1, x.shape[1] // LANES):
        t = op(t, x[:, c * LANES:(c + 1) * LANES])
    red = jnp.max if op is jnp.maximum else jnp.sum
    t = jnp.broadcast_to(red(t, axis=0, keepdims=True), (SUBLANES, LANES))
    sh = N_ROWS
    while sh < LANES:
        t = op(t, pltpu.roll(t, sh, 1))
        sh *= 2
    return t[0:1, :]


def _tile_lanes(per, width):
    return jnp.concatenate([per] * (width // LANES), axis=1)


def _page_rows(page_ref, kv):
    return page_ref[0, 0, :, kv, :, :].reshape(FLAT, HEAD_DIM).astype(BF16)


def _pair_lanes(rows):
    return jnp.concatenate([rows[:HALF], rows[HALF:]], axis=1)


def _flat_scores(qq, page_refs, s_scr):
    mask = _row_mask(HALF)
    for i, pref in enumerate(page_refs):
        out = _dot_nt(qq, _pair_lanes(_page_rows(pref, 0)))
        s_scr[i:i + 1, 0:HALF] = jnp.sum(jnp.where(mask, out[0:N_ROWS], 0.0), axis=0, keepdims=True)
        s_scr[i:i + 1, HALF:FLAT] = jnp.sum(jnp.where(mask, out[N_ROWS:], 0.0), axis=0, keepdims=True)


def _online_flat(sc, m_old, l_old):
    m_new = jnp.maximum(m_old, _periodic_reduce(sc, jnp.maximum))
    a = jnp.exp(m_old - m_new)
    p = jnp.exp(sc - _tile_lanes(m_new, FLAT))
    return p, m_new, l_old * a + _periodic_reduce(p, jnp.add), _periodic_to_col(a)


def _masked_lhs(p_row):
    pm = jnp.where(_row_mask(FLAT), jnp.broadcast_to(p_row, (N_ROWS, FLAT)), 0.0)
    return pm[:, :HALF], pm[:, HALF:]


def _swap_halves_of_8(x):
    parts = []
    for c in range(x.shape[1] // LANES):
        ch = x[:, c * LANES:(c + 1) * LANES]
        lane = lax.broadcasted_iota(jnp.int32, ch.shape, 1)
        parts.append(jnp.where((lane & 4) == 0, pltpu.roll(ch, LANES - 4, 1), pltpu.roll(ch, 4, 1)))
    return jnp.concatenate(parts, axis=1)


def _decode_init(q8d_ref, q8f_ref, knd_ref, knf_ref, vnd_ref, vnf_ref):
    sd = jnp.sum(q8d_ref[0].astype(F32) * knd_ref[0].astype(F32), axis=1, keepdims=True)
    sf = jnp.sum(q8f_ref[0].astype(F32) * knf_ref[0].astype(F32), axis=1, keepdims=True)
    ones = jnp.ones((1, LANES), F32)
    vd = vnd_ref[0].astype(F32)
    return (_col_to_periodic(sd), ones, jnp.concatenate([vd, vd], axis=0),
            _col_to_periodic(sf), ones, vnf_ref[0].astype(F32))


def _decode_update(state, bias, d_pages, f_pages, sd_scr, sf_scr):
    md, ld, accd, mf, lf, accf = state
    n_group = len(d_pages)

    p, mf, lf, a_col = _online_flat(sf_scr[...] + bias, mf, lf)
    lhs_a, lhs_b, rhs = [], [], []
    for i in range(n_group):
        pa, pb = _masked_lhs(p[i:i + 1, :])
        lhs_a.append(pa)
        lhs_b.append(pb)
        rhs.append(_pair_lanes(_page_rows(f_pages[i], 1)))
    lhs = jnp.concatenate([jnp.concatenate(lhs_a, axis=1), jnp.concatenate(lhs_b, axis=1)], axis=0)
    o = _dot(lhs.astype(BF16), jnp.concatenate(rhs, axis=0))
    accf = accf * a_col + o[0:N_ROWS, 0:HEAD_DIM] + o[N_ROWS:, HEAD_DIM:]

    p, md, ld, a_col = _online_flat(sd_scr[...], md, ld)
    p_sw = _swap_halves_of_8(p)
    blocks = [[], [], [], []]
    rhs = []
    for i in range(n_group):
        pa, pb = _masked_lhs(p[i:i + 1, :])
        qa, qb = _masked_lhs(p_sw[i:i + 1, :])
        for lst, blk in zip(blocks, (pa, qa, pb, qb)):
            lst.append(blk)
        rhs.append(_pair_lanes(_page_rows(d_pages[i], 1)))
    lhs = jnp.concatenate([jnp.concatenate(b, axis=1) for b in blocks], axis=0)
    o = _dot(lhs.astype(BF16), jnp.concatenate(rhs, axis=0))
    same = o[0:8, 0:HEAD_DIM] + o[16:24, HEAD_DIM:]
    other = o[8:16, 0:HEAD_DIM] + o[24:32, HEAD_DIM:]
    a_sw = pltpu.roll(jnp.broadcast_to(a_col, (N_ROWS, LANES)), N_ROWS // 2, 0)
    accd = jnp.concatenate([accd[0:N_ROWS] * a_col + same, accd[N_ROWS:] * a_sw + other], axis=0)
    return md, ld, accd, mf, lf, accf


def _decode_output(state, lam_ref, g_ref, lam_init):
    md, ld, accd, mf, lf, accf = state
    lp = lam_ref[...]
    lam = (jnp.exp(jnp.sum(lp[0:1] * lp[1:2], axis=1, keepdims=True))
           - jnp.exp(jnp.sum(lp[2:3] * lp[3:4], axis=1, keepdims=True)) + lam_init)
    l_col = _periodic_to_col(ld)
    l_sw = pltpu.roll(jnp.broadcast_to(l_col, (N_ROWS, LANES)), N_ROWS // 2, 0)
    o_same = accd[0:N_ROWS] / l_col
    o_other = accd[N_ROWS:] / l_sw
    row = lax.broadcasted_iota(jnp.int32, (N_ROWS, HEAD_DIM), 0)
    od = jnp.where(row < N_ROWS // 2, o_same - lam * o_other, o_other - lam * o_same)
    ss = jnp.broadcast_to(jnp.sum(od * od, axis=1, keepdims=True), (N_ROWS, LANES))
    ms = (ss + pltpu.roll(ss, N_ROWS // 2, 0)) * (1.0 / (2 * HEAD_DIM))
    od = od * lax.rsqrt(ms + EPS) * g_ref[...] * (1.0 - lam_init)
    return jnp.concatenate([od, accf / _periodic_to_col(lf)], axis=0)


def _merge_kernel(od_ref, of_ref, x_ref, wd_ref, wf_ref, g_ref, o_ref):
    a = _dot(od_ref[...], wd_ref[...]) + _dot(of_ref[...], wf_ref[...])
    o_ref[...] = x_ref[...] + _rms(a, g_ref[...])


def _merge(od, of, x, w_od, w_of, g, tm):
    n, d = x.shape
    half = od.shape[1]
    return pl.pallas_call(
        _merge_kernel,
        grid=(n // tm,),
        in_specs=[
            pl.BlockSpec((tm, half), lambda i: (i, 0)),
            pl.BlockSpec((tm, half), lambda i: (i, 0)),
            pl.BlockSpec((tm, d), lambda i: (i, 0)),
            pl.BlockSpec((half, d), lambda i: (0, 0)),
            pl.BlockSpec((half, d), lambda i: (0, 0)),
            pl.BlockSpec((1, d), lambda i: (0, 0)),
        ],
        out_specs=pl.BlockSpec((tm, d), lambda i: (i, 0)),
        out_shape=jax.ShapeDtypeStruct((n, d), F32),
        compiler_params=_cparams(1),
        name="merge",
    )(od, of, x, w_od, w_of, g)


def _ffn_prologue(x_ref, gpre_ref, h_scr, acc_scr):
    @pl.when(pl.program_id(1) == 0)
    def _():
        h_scr[...] = _rms(x_ref[...], gpre_ref[...]).astype(BF16)
        acc_scr[...] = jnp.zeros_like(acc_scr)


def _ffn_main(wg_ref, wu_ref, wd_ref, h_scr, acc_scr):
    hb = h_scr[...]
    gate = _dot(hb, wg_ref[...])
    up = _dot(hb, wu_ref[...])
    u = gate * jax.nn.sigmoid(gate) * up
    acc_scr[...] += _dot(u.astype(BF16), wd_ref[...])


def _ffn_epilogue(x_ref, gpost_ref, o_ref, acc_scr):
    @pl.when(pl.program_id(1) == pl.num_programs(1) - 1)
    def _():
        o_ref[...] = x_ref[...] + _rms(acc_scr[...], gpost_ref[...])


def _ffn_kernel(x_ref, gpre_ref, wg_ref, wu_ref, wd_ref, gpost_ref, o_ref, h_scr, acc_scr):
    _ffn_prologue(x_ref, gpre_ref, h_scr, acc_scr)
    _ffn_main(wg_ref, wu_ref, wd_ref, h_scr, acc_scr)
    _ffn_epilogue(x_ref, gpost_ref, o_ref, acc_scr)


def _ffn_specs(tm, d, tf, imap):
    return [
        pl.BlockSpec((tm, d), imap(lambda i, j: (i, 0))),
        pl.BlockSpec((1, d), imap(lambda i, j: (0, 0))),
        pl.BlockSpec((d, tf), imap(lambda i, j: (0, j))),
        pl.BlockSpec((d, tf), imap(lambda i, j: (0, j))),
        pl.BlockSpec((tf, d), imap(lambda i, j: (j, 0))),
        pl.BlockSpec((1, d), imap(lambda i, j: (0, 0))),
    ]


def _ffn(x, g_pre, w_gate, w_up, w_down, g_post, tm, tf):
    n, d = x.shape
    f = w_gate.shape[1]
    return pl.pallas_call(
        _ffn_kernel,
        grid=(n // tm, f // tf),
        in_specs=_ffn_specs(tm, d, tf, lambda fn: fn),
        out_specs=pl.BlockSpec((tm, d), lambda i, j: (i, 0)),
        out_shape=jax.ShapeDtypeStruct((n, d), F32),
        scratch_shapes=[pltpu.VMEM((tm, d), BF16), pltpu.VMEM((tm, d), F32)],
        compiler_params=_cparams(2),
        name="ffn",
    )(x, g_pre, w_gate, w_up, w_down, g_post)


def _ffn_decode_kernel(*refs, n_group, steps_per_seq, n_decode_steps, lam_init):
    pt_ref = refs[0]
    x_ref, gpre_ref, wg_ref, wu_ref, wd_ref, gpost_ref = refs[1:7]
    (qqd_ref, qqf_ref, q8d_ref, q8f_ref, knd_ref, knf_ref, vnd_ref, vnf_ref,
     lam_ref, g_ref, bias_ref) = refs[7:18]
    d_pages = refs[18:18 + n_group]
    f_pages = refs[18 + n_group:18 + 2 * n_group]
    y_ref, os_ref = refs[18 + 2 * n_group:20 + 2 * n_group]
    h_scr, acc_scr, sd_scr, sf_scr = refs[20 + 2 * n_group:24 + 2 * n_group]
    state_scr = refs[24 + 2 * n_group:]
    del pt_ref
    t = pl.program_id(0) * pl.num_programs(1) + pl.program_id(1)
    _ffn_prologue(x_ref, gpre_ref, h_scr, acc_scr)

    @pl.when(t == 0)
    def _():
        for scr in state_scr:
            scr[...] = jnp.zeros_like(scr)

    @pl.when(t < n_decode_steps)
    def _():
        _flat_scores(qqf_ref[0], f_pages, sf_scr)
        _flat_scores(qqd_ref[0], d_pages, sd_scr)
        _ffn_main(wg_ref, wu_ref, wd_ref, h_scr, acc_scr)
        first = (t % steps_per_seq) == 0
        cur = tuple(scr[...] for scr in state_scr)
        init = _decode_init(q8d_ref, q8f_ref, knd_ref, knf_ref, vnd_ref, vnf_ref)
        start = tuple(jnp.where(first, a, b) for a, b in zip(init, cur))
        new = _decode_update(start, bias_ref[0, 0], d_pages, f_pages, sd_scr, sf_scr)
        for scr, val in zip(state_scr, new):
            scr[...] = val
        os_ref[0] = _decode_output(new, lam_ref, g_ref, lam_init)

    @pl.when(t >= n_decode_steps)
    def _():
        _ffn_main(wg_ref, wu_ref, wd_ref, h_scr, acc_scr)

    _ffn_epilogue(x_ref, gpost_ref, y_ref, acc_scr)


def _ffn_decode(x, g_pre, w_gate, w_up, w_down, g_post, tm, tf,
                page_table, cache_d, cache_f, bias, qq_d, qq_f, q8_d, q8_f,
                kn_d, kn_f, vn_d, vn_f, lam_params, g_rows, n_group, lam_init):
    n, d = x.shape
    f = w_gate.shape[1]
    nj = f // tf
    db, n_pages = page_table.shape
    sps = n_pages // n_group
    n_dec = db * sps
    assert (n // tm) * nj >= n_dec

    def seq_of(i, j):
        t = jnp.minimum(i * nj + j, n_dec - 1)
        return t // sps, t % sps

    def seq3(shape):
        return pl.BlockSpec((1,) + shape, lambda i, j, pt: (seq_of(i, j)[0], 0, 0))

    def page_spec(k):
        def imap(i, j, pt):
            b, g = seq_of(i, j)
            return (0, pt[b, g * n_group + k], 0, 0, 0, 0)
        return pl.BlockSpec((1, 1, PAGE_SIZE, 2, N_ROWS, HEAD_DIM), imap)

    in_specs = _ffn_specs(tm, d, tf, lambda fn: (lambda i, j, pt: fn(i, j)))
    in_specs += [
        seq3((2 * N_ROWS, 2 * HEAD_DIM)), seq3((2 * N_ROWS, 2 * HEAD_DIM)),
        seq3((N_ROWS, HEAD_DIM)), seq3((N_ROWS, HEAD_DIM)),
        seq3((N_ROWS, HEAD_DIM)), seq3((N_ROWS, HEAD_DIM)),
        seq3((N_ROWS, HEAD_DIM)), seq3((N_ROWS, HEAD_DIM)),
        pl.BlockSpec((4, HEAD_DIM), lambda i, j, pt: (0, 0)),
        pl.BlockSpec((N_ROWS, HEAD_DIM), lambda i, j, pt: (0, 0)),
        pl.BlockSpec((1, 1, n_group, FLAT), lambda i, j, pt: seq_of(i, j) + (0, 0)),
    ]
    in_specs += [page_spec(k) for k in range(n_group)] * 2
    return pl.pallas_call(
        functools.partial(_ffn_decode_kernel, n_group=n_group, steps_per_seq=sps,
                          n_decode_steps=n_dec, lam_init=lam_init),
        grid_spec=pltpu.PrefetchScalarGridSpec(
            num_scalar_prefetch=1,
            grid=(n // tm, nj),
            in_specs=in_specs,
            out_specs=[
                pl.BlockSpec((tm, d), lambda i, j, pt: (i, 0)),
                pl.BlockSpec((1, 2 * N_ROWS, HEAD_DIM), lambda i, j, pt: (seq_of(i, j)[0], 0, 0)),
            ],
            scratch_shapes=[
                pltpu.VMEM((tm, d), BF16), pltpu.VMEM((tm, d), F32),
                pltpu.VMEM((n_group, FLAT), F32), pltpu.VMEM((n_group, FLAT), F32),
                pltpu.VMEM((1, LANES), F32), pltpu.VMEM((1, LANES), F32),
                pltpu.VMEM((2 * N_ROWS, HEAD_DIM), F32),
                pltpu.VMEM((1, LANES), F32), pltpu.VMEM((1, LANES), F32),
                pltpu.VMEM((N_ROWS, HEAD_DIM), F32),
            ],
        ),
        out_shape=[
            jax.ShapeDtypeStruct((n, d), F32),
            jax.ShapeDtypeStruct((db, 2 * N_ROWS, HEAD_DIM), F32),
        ],
        compiler_params=_cparams(2, LARGE_VMEM_LIMIT),
        name="ffn_decode",
    )(page_table, x, g_pre, w_gate, w_up, w_down, g_post,
      qq_d, qq_f, q8_d, q8_f, kn_d, kn_f, vn_d, vn_f, lam_params, g_rows,
      bias.reshape(db, sps, n_group, FLAT), *([cache_d] * n_group), *([cache_f] * n_group))


def _proj_tables(pos, scale):
    half = HEAD_DIM // 2
    inv = ROPE_THETA ** (-jnp.arange(half, dtype=F32) / half)
    ang = pos.astype(F32)[:, None] * inv[None, :]
    cos, sin = jnp.cos(ang), jnp.sin(ang)
    cos, sin = jnp.concatenate([cos, cos], axis=1), jnp.concatenate([-sin, sin], axis=1)
    one, zero = jnp.ones_like(cos), jnp.zeros_like(cos)
    return (jnp.stack([cos * scale, cos, one * scale, one]),
            jnp.stack([sin * scale, sin, zero, zero]))


def _tile_rows(n, target):
    t = min(n, target)
    while n % t:
        t //= 2
    return t


def _as_key_tiles(cache):
    _, pool, page, _, rows, width = cache.shape
    halves = width // HEAD_DIM
    c = cache.reshape(1, pool, page, 2, rows, halves, HEAD_DIM)
    return jnp.transpose(c, (0, 1, 2, 3, 5, 4, 6)).reshape(1, pool, page, 2, rows * halves, HEAD_DIM)


def _rows_half_head(v):
    n = v.shape[0]
    return jnp.transpose(v.reshape(n, 4, 2, HEAD_DIM), (0, 2, 1, 3)).reshape(n, N_ROWS, HEAD_DIM)


def kernel(x_prompt, x_sample, cache_diff_kv, cache_fox_kv, cache_fox_logf, page_table, meta_tokens, g_attn_pre, w_in, b_f, diff_lambda, g_subln, w_o, g_attn_post, g_ffn_pre, w_gate, w_up, w_down, g_ffn_post):
    batch, seq, d_model = x_prompt.shape
    db, t_new, _ = x_sample.shape
    assert t_new == 1 and w_in.shape[0] == 1
    n_pages = page_table.shape[1]
    past = n_pages * PAGE_SIZE
    lam_init = 0.8 - 0.6 * math.exp(-0.3 * 0)
    n_p = batch * seq
    n_main = 6 * HEAD_COLS

    w_main = w_in[0][:, :n_main].astype(BF16)
    w_fl = jnp.pad(w_in[0][:, n_main:], ((0, 0), (0, LANES - N_ROWS))).astype(BF16)
    b_fl = jnp.pad(b_f[0].astype(F32), (0, LANES - N_ROWS))[None, :]
    g_pre = g_attn_pre[0][None, :].astype(F32)

    tm_p = _tile_rows(seq, 1024)
    xp = x_prompt.reshape(n_p, d_model)
    scale = HEAD_DIM ** -0.5
    tab_a, tab_b = _proj_tables(N_META + jnp.arange(seq), scale * LOG2E)
    qkv_p, kv_p, lf_p = _proj(xp, g_pre, w_main, w_fl, b_fl, tab_a, tab_b, tm_p)
    kvd_p, kvf_p = kv_p[:, :2 * HEAD_COLS], kv_p[:, 2 * HEAD_COLS:]

    n_e = N_META + db
    x_e = jnp.concatenate([meta_tokens.astype(F32), x_sample.reshape(db, d_model)], axis=0)
    pos_e = jnp.concatenate([jnp.arange(N_META), jnp.full((db,), past)])
    tab_a, tab_b = _proj_tables(pos_e, scale)
    qkv_e, kv_e, lf_e = _proj(x_e, g_pre, w_main, w_fl, b_fl, tab_a, tab_b, n_e)
    kvd_e, kvf_e = kv_e[:, :2 * HEAD_COLS], kv_e[:, 2 * HEAD_COLS:]

    tc = _tile_rows(seq, 256)
    dcum_p, dcum_m = _cumsum(lf_p, lf_e[:N_META], batch, tc, LOG2E)
    tq = _tile_rows(seq, 512)
    tk = tq
    ck = jnp.transpose(dcum_p[:, :N_ROWS].reshape(batch, seq, N_ROWS), (0, 2, 1))
    ck = ck.reshape(batch, N_ROWS, seq // tk, tk)
    ckm = jnp.pad(dcum_m[:N_META, :N_ROWS].T, ((0, 0), (0, LANES - N_META)))
    qkv_m = jnp.pad(qkv_e[:N_META], ((0, LANES - N_META), (0, 0)))
    lam_params = diff_lambda[0].astype(F32)
    g_sub = g_subln[0].astype(F32)[None, :]
    od_p, of_p = _prompt_attention(qkv_p, qkv_m, dcum_p, ck, ckm, lam_params, g_sub,
                                   batch, seq, tq, tk, lam_init)

    qkv_s = qkv_e[N_META:]
    lf_s = lf_e[N_META:, :N_ROWS]
    cols = lambda a, j: a[:, j * HEAD_COLS:(j + 1) * HEAD_COLS]
    q8_d = _rows_half_head(cols(qkv_s, 0))
    kn_d = _rows_half_head(cols(qkv_s, 1))
    vn_d = _rows_half_head(cols(qkv_s, 2))
    q8_f = cols(qkv_s, 3).reshape(db, N_ROWS, HEAD_DIM)
    kn_f = cols(qkv_s, 4).reshape(db, N_ROWS, HEAD_DIM)
    vn_f = cols(qkv_s, 5).reshape(db, N_ROWS, HEAD_DIM)

    def paired(q8):
        z = jnp.zeros_like(q8)
        return jnp.concatenate([jnp.concatenate([q8, z], axis=2), jnp.concatenate([z, q8], axis=2)], axis=1)

    lf_flat = cache_fox_logf[0][page_table].astype(F32).reshape(db, n_pages, FLAT)
    ct_flat = jnp.tile(lf_s, (1, PAGE_SIZE))[:, None, :]
    bias = _sample_bias(lf_flat, ct_flat)
    g_rows = jnp.repeat(g_subln[0].astype(F32).reshape(2, 1, HEAD_DIM), 4, axis=1).reshape(N_ROWS, HEAD_DIM)
    n_group = 4 if n_pages % 4 == 0 else 1

    w_od = w_o[0][:HEAD_COLS].astype(BF16)
    w_of = w_o[0][HEAD_COLS:].astype(BF16)
    g_post = g_attn_post[0][None, :].astype(F32)
    g_fpre = g_ffn_pre[0][None, :].astype(F32)
    g_fpost = g_ffn_post[0][None, :].astype(F32)
    wg, wu, wd = w_gate[0].astype(BF16), w_up[0].astype(BF16), w_down[0].astype(BF16)
    x1_p = _merge(od_p, of_p, xp, w_od, w_of, g_post, _tile_rows(n_p, 256))
    y_p, o_s = _ffn_decode(x1_p, g_fpre, wg, wu, wd, g_fpost, _tile_rows(n_p, 512), 256,
                           page_table, _as_key_tiles(cache_diff_kv), _as_key_tiles(cache_fox_kv), bias,
                           paired(q8_d), paired(q8_f), q8_d, q8_f, kn_d, kn_f, vn_d, vn_f,
                           lam_params, g_rows, n_group, lam_init)
    od_s = jnp.transpose(o_s[:, :N_ROWS].reshape(db, 2, 4, HEAD_DIM), (0, 2, 1, 3)).reshape(db, HEAD_COLS)
    of_s = o_s[:, N_ROWS:].reshape(db, HEAD_COLS)
    xs = x_sample.reshape(db, d_model)
    x1_s = _merge(od_s.astype(BF16), of_s.astype(BF16), xs, w_od, w_of, g_post, db)
    y_s = _ffn(x1_s, g_fpre, wg, wu, wd, g_fpost, db, 512)

    def with_meta(rows_e, rows_p, tail):
        m = jnp.broadcast_to(rows_e[:N_META][None], (batch, N_META) + rows_e.shape[1:])
        full = jnp.concatenate([m, rows_p.reshape((batch, seq) + rows_p.shape[1:])], axis=1)
        return full.reshape((1, batch, N_META + seq) + tail)

    dkv_p = with_meta(kvd_e, kvd_p, (2, 4, 2 * HEAD_DIM)).astype(cache_diff_kv.dtype)
    fkv_p = with_meta(kvf_e, kvf_p, (2, N_ROWS, HEAD_DIM)).astype(cache_fox_kv.dtype)
    flf_p = with_meta(lf_e[:, :N_ROWS], lf_p[:, :N_ROWS], (N_ROWS,)).astype(cache_fox_logf.dtype)
    dkv_s = kvd_e[N_META:].reshape(1, db, 1, 2, 4, 2 * HEAD_DIM).astype(cache_diff_kv.dtype)
    fkv_s = kvf_e[N_META:].reshape(1, db, 1, 2, N_ROWS, HEAD_DIM).astype(cache_fox_kv.dtype)
    flf_s = lf_s.reshape(1, db, 1, N_ROWS).astype(cache_fox_logf.dtype)
    return (y_p.reshape(batch, seq, d_model), y_s.reshape(db, 1, d_model),
            dkv_p, fkv_p, flf_p, dkv_s, fkv_s, flf_s)
```

```python
import functools
import math

import jax
import jax.numpy as jnp
from jax import lax
from jax.experimental import pallas as pl
from jax.experimental.pallas import tpu as pltpu

F32 = jnp.float32
BF16 = jnp.bfloat16

HEAD_DIM = 128
N_META = 16
PAGE_SIZE = 128
ROPE_THETA = 10000.0
EPS = 1e-6
LANES = 128
SUBLANES = 8
N_ROWS = 8
HEAD_COLS = 1024
VMEM_LIMIT = 52 * 1024 * 1024
LARGE_VMEM_LIMIT = 58 * 1024 * 1024
LOG2E = math.log2(math.e)


def _cparams(n_axes, vmem=VMEM_LIMIT):
    return pltpu.CompilerParams(dimension_semantics=("arbitrary",) * n_axes,
                                vmem_limit_bytes=vmem)


def _rms(x, g):
    return x * lax.rsqrt(jnp.mean(x * x, axis=-1, keepdims=True) + EPS) * g


def _dot(a, b):
    return jnp.dot(a, b, preferred_element_type=F32)


def _dot_nt(a, b):
    return lax.dot_general(a, b, (((1,), (1,)), ((), ())), preferred_element_type=F32)


def _split3(x):
    hi = x.astype(BF16)
    r1 = x - hi.astype(F32)
    mid = r1.astype(BF16)
    lo = (r1 - mid.astype(F32)).astype(BF16)
    return hi, mid, lo


def _dot3_rhs(m, x):
    hi, mid, lo = _split3(x)
    return _dot(m, hi) + _dot(m, mid) + _dot(m, lo)


def _proj_kernel(x_ref, g_ref, w_ref, wfl_ref, bf_ref, a_ref, b_ref,
                 qkv_ref, kvd_ref, kvf_ref, logf_ref, h_scr):
    j = pl.program_id(1)

    @pl.when(j == 0)
    def _():
        hb = _rms(x_ref[...], g_ref[...]).astype(BF16)
        h_scr[...] = hb
        zl = _dot(hb, wfl_ref[...]) + bf_ref[...]
        logf_ref[...] = jnp.minimum(zl, 0.0) - jnp.log1p(jnp.exp(-jnp.abs(zl)))

    z = _dot(h_scr[...], w_ref[...])
    a, b = a_ref[0], b_ref[0]
    parts = []
    for gi in range(HEAD_COLS // HEAD_DIM):
        zg = z[:, gi * HEAD_DIM:(gi + 1) * HEAD_DIM]
        parts.append(zg * a + pltpu.roll(zg, HEAD_DIM // 2, 1) * b)
    out = jnp.concatenate(parts, axis=1)
    qkv_ref[...] = out.astype(BF16)

    @pl.when(jnp.logical_or(j == 1, j == 2))
    def _():
        kvd_ref[...] = out

    @pl.when(j >= 4)
    def _():
        kvf_ref[...] = out


def _proj(x, g, w_main, w_fl, b_fl, tab_a, tab_b, tm):
    n, d = x.shape
    n_col = w_main.shape[1] // HEAD_COLS
    tab_blocks = tab_a.shape[1] // tm
    tab_spec = pl.BlockSpec((1, tm, HEAD_DIM), lambda i, j: (j, i % tab_blocks, 0))
    return pl.pallas_call(
        _proj_kernel,
        grid=(n // tm, n_col),
        in_specs=[
            pl.BlockSpec((tm, d), lambda i, j: (i, 0)),
            pl.BlockSpec((1, d), lambda i, j: (0, 0)),
            pl.BlockSpec((d, HEAD_COLS), lambda i, j: (0, j)),
            pl.BlockSpec((d, LANES), lambda i, j: (0, 0)),
            pl.BlockSpec((1, LANES), lambda i, j: (0, 0)),
            tab_spec, tab_spec,
        ],
        out_specs=[
            pl.BlockSpec((tm, HEAD_COLS), lambda i, j: (i, j)),
            pl.BlockSpec((tm, HEAD_COLS), lambda i, j: (i, jnp.clip(j - 1, 0, 1))),
            pl.BlockSpec((tm, HEAD_COLS), lambda i, j: (i, jnp.clip(j - 4, 0, 1))),
            pl.BlockSpec((tm, LANES), lambda i, j: (i, 0)),
        ],
        out_shape=[
            jax.ShapeDtypeStruct((n, n_col * HEAD_COLS), BF16),
            jax.ShapeDtypeStruct((n, 2 * HEAD_COLS), F32),
            jax.ShapeDtypeStruct((n, 2 * HEAD_COLS), F32),
            jax.ShapeDtypeStruct((n, LANES), F32),
        ],
        scratch_shapes=[pltpu.VMEM((tm, d), BF16)],
        compiler_params=_cparams(2, LARGE_VMEM_LIMIT),
        name="proj",
    )(x, g, w_main, w_fl, b_fl, tab_a, tab_b)


def _tri(n):
    r = lax.broadcasted_iota(jnp.int32, (n, n), 0)
    c = lax.broadcasted_iota(jnp.int32, (n, n), 1)
    return jnp.where(c <= r, 1.0, 0.0).astype(BF16)


def _cumsum_kernel(lf_ref, lfm_ref, out_ref, outm_ref, *, tc, scale):
    tri = _tri(tc)
    dm = _dot3_rhs(tri, lfm_ref[...])

    @pl.when(pl.program_id(0) == 0)
    def _():
        outm_ref[...] = dm * scale

    carry = dm[N_META - 1:N_META, :]
    for c in range(lf_ref.shape[0] // tc):
        dc = _dot3_rhs(tri, lf_ref[c * tc:(c + 1) * tc, :]) + carry
        out_ref[c * tc:(c + 1) * tc, :] = dc * scale
        carry = dc[tc - 1:tc, :]


def _cumsum(lf_p, lf_m, batch, tc, scale):
    n = lf_p.shape[0]
    seq = n // batch
    return pl.pallas_call(
        functools.partial(_cumsum_kernel, tc=tc, scale=scale),
        grid=(batch,),
        in_specs=[
            pl.BlockSpec((seq, LANES), lambda b: (b, 0)),
            pl.BlockSpec((tc, LANES), lambda b: (0, 0)),
        ],
        out_specs=[
            pl.BlockSpec((seq, LANES), lambda b: (b, 0)),
            pl.BlockSpec((tc, LANES), lambda b: (0, 0)),
        ],
        out_shape=[
            jax.ShapeDtypeStruct((n, LANES), F32),
            jax.ShapeDtypeStruct((tc, LANES), F32),
        ],
        compiler_params=_cparams(1),
        name="cumsum",
    )(lf_p, jnp.pad(lf_m, ((0, tc - N_META), (0, 0))))


def _softmax_init(s, v):
    m = jnp.max(s, axis=1, keepdims=True)
    p = jnp.exp2(s - m)
    return m, jnp.sum(p, axis=1, keepdims=True), _dot(p.astype(BF16), v)


def _softmax_update(state, s, v):
    m, l, acc = state
    m_new = jnp.maximum(m, jnp.max(s, axis=1, keepdims=True))
    a = jnp.exp2(m - m_new)
    p = jnp.exp2(s - m_new)
    return (m_new, l * a + jnp.sum(p, axis=1, keepdims=True),
            acc * a + _dot(p.astype(BF16), v))


def _attend(q, k_of, v_of, bias_of, km, vm, bias_m, qi, tq, tk):
    col_m = lax.broadcasted_iota(jnp.int32, (tq, LANES), 1)
    s = _dot_nt(q, km)
    if bias_m is not None:
        s = s + bias_m
    state = _softmax_init(jnp.where(col_m < N_META, s, -jnp.inf), vm)

    def body(j, st):
        s = _dot_nt(q, k_of(j))
        b = bias_of(j)
        if b is not None:
            s = s + b
        return _softmax_update(st, s, v_of(j))

    state = lax.fori_loop(0, qi * (tq // tk), body, state)
    row = lax.broadcasted_iota(jnp.int32, (tq, tk), 0)
    col = lax.broadcasted_iota(jnp.int32, (tq, tk), 1)
    for dj in range(tq // tk):
        j = qi * (tq // tk) + dj
        s = _dot_nt(q, k_of(j))
        b = bias_of(j)
        if b is not None:
            s = s + b
        s = jnp.where(col + dj * tk <= row, s, -jnp.inf)
        state = _softmax_update(state, s, v_of(j))
    return state


def _fox_attn_kernel(q_ref, k_ref, v_ref, km_ref, vm_ref, cq_ref, ck_ref, ckm_ref, o_ref,
                     *, tq, tk, n_heads):
    qi = pl.program_id(1)
    for h in range(n_heads):
        sl = slice(h * HEAD_DIM, (h + 1) * HEAD_DIM)
        cq = cq_ref[:, h:h + 1]

        def rows(j):
            return pl.ds(pl.multiple_of(j * tk, tk), tk)

        _, l, acc = _attend(
            q_ref[:, sl],
            lambda j: k_ref[rows(j), sl],
            lambda j: v_ref[rows(j), sl],
            lambda j: cq - ck_ref[0, h, pl.ds(j, 1), :],
            km_ref[:, sl], vm_ref[:, sl], cq - ckm_ref[h:h + 1, :],
            qi, tq, tk)
        o_ref[:, sl] = (acc / l).astype(BF16)


def _diff_attn_kernel(q_ref, k_ref, v_ref, km_ref, vm_ref, lam_ref, g_ref, o_ref,
                      *, tq, tk, n_heads, lam_init):
    qi = pl.program_id(1)
    lp = lam_ref[...]
    lam = (jnp.exp(jnp.sum(lp[0:1] * lp[1:2], axis=1, keepdims=True))
           - jnp.exp(jnp.sum(lp[2:3] * lp[3:4], axis=1, keepdims=True)) + lam_init)
    for h in range(n_heads):
        vsl = slice(h * 2 * HEAD_DIM, (h + 1) * 2 * HEAD_DIM)

        def rows(j):
            return pl.ds(pl.multiple_of(j * tk, tk), tk)

        outs = []
        for c in range(2):
            sl = slice((2 * h + c) * HEAD_DIM, (2 * h + c + 1) * HEAD_DIM)
            _, l, acc = _attend(
                q_ref[:, sl],
                lambda j, sl=sl: k_ref[rows(j), sl],
                lambda j: v_ref[rows(j), vsl],
                lambda j: None,
                km_ref[:, sl], vm_ref[:, vsl], None,
                qi, tq, tk)
            outs.append(acc / l)
        od = outs[0] - lam * outs[1]
        o_ref[:, vsl] = (_rms(od, g_ref[...]) * (1.0 - lam_init)).astype(BF16)


def _prompt_attention(qkv_p, qkv_m, cq, ck, ckm, lam_params, g_subln, batch, seq, tq, tk, lam_init):
    n = qkv_p.shape[0]
    nq = seq // tq
    pad_m = qkv_m.shape[0]

    def qmap(col):
        return lambda b, qi: (b * nq + qi, col)

    def kmap(col):
        return lambda b, qi: (b, col)

    def mmap(col):
        return lambda b, qi: (0, col)

    fox = pl.pallas_call(
        functools.partial(_fox_attn_kernel, tq=tq, tk=tk, n_heads=HEAD_COLS // HEAD_DIM),
        grid=(batch, nq),
        in_specs=[
            pl.BlockSpec((tq, HEAD_COLS), qmap(3)),
            pl.BlockSpec((seq, HEAD_COLS), kmap(4)),
            pl.BlockSpec((seq, HEAD_COLS), kmap(5)),
            pl.BlockSpec((pad_m, HEAD_COLS), mmap(4)),
            pl.BlockSpec((pad_m, HEAD_COLS), mmap(5)),
            pl.BlockSpec((tq, LANES), qmap(0)),
            pl.BlockSpec((1, N_ROWS, seq // tk, tk), lambda b, qi: (b, 0, 0, 0)),
            pl.BlockSpec((N_ROWS, LANES), lambda b, qi: (0, 0)),
        ],
        out_specs=pl.BlockSpec((tq, HEAD_COLS), qmap(0)),
        out_shape=jax.ShapeDtypeStruct((n, HEAD_COLS), BF16),
        compiler_params=_cparams(2),
        name="attn_fox",
    )(qkv_p, qkv_p, qkv_p, qkv_m, qkv_m, cq, ck, ckm)

    diff = pl.pallas_call(
        functools.partial(_diff_attn_kernel, tq=tq, tk=tk,
                          n_heads=HEAD_COLS // (2 * HEAD_DIM), lam_init=lam_init),
        grid=(batch, nq),
        in_specs=[
            pl.BlockSpec((tq, HEAD_COLS), qmap(0)),
            pl.BlockSpec((seq, HEAD_COLS), kmap(1)),
            pl.BlockSpec((seq, HEAD_COLS), kmap(2)),
            pl.BlockSpec((pad_m, HEAD_COLS), mmap(1)),
            pl.BlockSpec((pad_m, HEAD_COLS), mmap(2)),
            pl.BlockSpec((4, HEAD_DIM), lambda b, qi: (0, 0)),
            pl.BlockSpec((1, 2 * HEAD_DIM), lambda b, qi: (0, 0)),
        ],
        out_specs=pl.BlockSpec((tq, HEAD_COLS), qmap(0)),
        out_shape=jax.ShapeDtypeStruct((n, HEAD_COLS), BF16),
        compiler_params=_cparams(2),
        name="attn_diff",
    )(qkv_p, qkv_p, qkv_p, qkv_m, qkv_m, lam_params, g_subln)
    return diff, fox


FLAT = PAGE_SIZE * N_ROWS


def _sample_bias_kernel(lf_ref, ct_ref, out_ref, t_scr, p_scr):
    @pl.when(pl.program_id(0) == 0)
    def _():
        r = lax.broadcasted_iota(jnp.int32, (FLAT, FLAT), 0)
        c = lax.broadcasted_iota(jnp.int32, (FLAT, FLAT), 1)
        same = (r % N_ROWS) == (c % N_ROWS)
        p_scr[...] = jnp.where(same, 1.0, 0.0).astype(BF16)
        t_scr[...] = jnp.where(jnp.logical_and(same, r // N_ROWS > c // N_ROWS), 1.0, 0.0).astype(BF16)

    x = lf_ref[0]
    n_pages = x.shape[0]
    hi, mid, lo = _split3(x)
    within = _dot(hi, t_scr[...]) + _dot(mid, t_scr[...]) + _dot(lo, t_scr[...])
    totals = _dot(hi, p_scr[...]) + _dot(mid, p_scr[...]) + _dot(lo, p_scr[...])
    pr = lax.broadcasted_iota(jnp.int32, (n_pages, n_pages), 0)
    pc = lax.broadcasted_iota(jnp.int32, (n_pages, n_pages), 1)
    later = jnp.where(pc > pr, 1.0, 0.0).astype(BF16)
    out_ref[0] = within + _dot3_rhs(later, totals) + ct_ref[0]


def _sample_bias(lf_flat, ct_flat):
    db, n_pages, _ = lf_flat.shape
    return pl.pallas_call(
        _sample_bias_kernel,
        grid=(db,),
        in_specs=[
            pl.BlockSpec((1, n_pages, FLAT), lambda b: (b, 0, 0)),
            pl.BlockSpec((1, 1, FLAT), lambda b: (b, 0, 0)),
        ],
        out_specs=pl.BlockSpec((1, n_pages, FLAT), lambda b: (b, 0, 0)),
        out_shape=jax.ShapeDtypeStruct((db, n_pages, FLAT), F32),
        scratch_shapes=[pltpu.VMEM((FLAT, FLAT), BF16), pltpu.VMEM((FLAT, FLAT), BF16)],
        compiler_params=_cparams(1),
        name="sample_bias",
    )(lf_flat, ct_flat)


HALF = FLAT // 2


def _row_mask(width):
    lane = lax.broadcasted_iota(jnp.int32, (N_ROWS, width), 1)
    sub = lax.broadcasted_iota(jnp.int32, (N_ROWS, width), 0)
    return (lane % N_ROWS) == sub


def _col_to_periodic(col):
    return jnp.sum(jnp.where(_row_mask(LANES), jnp.broadcast_to(col, (N_ROWS, LANES)), 0.0),
                   axis=0, keepdims=True)


def _periodic_to_col(per):
    lane = lax.broadcasted_iota(jnp.int32, (N_ROWS, LANES), 1)
    sub = lax.broadcasted_iota(jnp.int32, (N_ROWS, LANES), 0)
    return jnp.sum(jnp.where(lane == sub, jnp.broadcast_to(per, (N_ROWS, LANES)), 0.0),
                   axis=1, keepdims=True)


def _periodic_reduce(x, op):
    t = x[:, 0:LANES]
    for c in range(1, x.shape[1] // LANES):
        t = op(t, x[:, c * LANES:(c + 1) * LANES])
    red = jnp.max if op is jnp.maximum else jnp.sum
    t = jnp.broadcast_to(red(t, axis=0, keepdims=True), (SUBLANES, LANES))
    sh = N_ROWS
    while sh < LANES:
        t = op(t, pltpu.roll(t, sh, 1))
        sh *= 2
    return t[0:1, :]


def _tile_lanes(per, width):
    return jnp.concatenate([per] * (width // LANES), axis=1)


def _page_rows(page_ref, kv):
    return page_ref[0, 0, :, kv, :, :].reshape(FLAT, HEAD_DIM).astype(BF16)


def _pair_lanes(rows):
    return jnp.concatenate([rows[:HALF], rows[HALF:]], axis=1)


def _flat_scores(qq, page_refs, s_scr):
    mask = _row_mask(HALF)
    for i, pref in enumerate(page_refs):
        out = _dot_nt(qq, _pair_lanes(_page_rows(pref, 0)))
        s_scr[i:i + 1, 0:HALF] = jnp.sum(jnp.where(mask, out[0:N_ROWS], 0.0), axis=0, keepdims=True)
        s_scr[i:i + 1, HALF:FLAT] = jnp.sum(jnp.where(mask, out[N_ROWS:], 0.0), axis=0, keepdims=True)


def _online_flat(sc, m_old, l_old):
    m_new = jnp.maximum(m_old, _periodic_reduce(sc, jnp.maximum))
    a = jnp.exp(m_old - m_new)
    p = jnp.exp(sc - _tile_lanes(m_new, FLAT))
    return p, m_new, l_old * a + _periodic_reduce(p, jnp.add), _periodic_to_col(a)


def _masked_lhs(p_row):
    pm = jnp.where(_row_mask(FLAT), jnp.broadcast_to(p_row, (N_ROWS, FLAT)), 0.0)
    return pm[:, :HALF], pm[:, HALF:]


def _swap_halves_of_8(x):
    parts = []
    for c in range(x.shape[1] // LANES):
        ch = x[:, c * LANES:(c + 1) * LANES]
        lane = lax.broadcasted_iota(jnp.int32, ch.shape, 1)
        parts.append(jnp.where((lane & 4) == 0, pltpu.roll(ch, LANES - 4, 1), pltpu.roll(ch, 4, 1)))
    return jnp.concatenate(parts, axis=1)


def _decode_init(q8d_ref, q8f_ref, knd_ref, knf_ref, vnd_ref, vnf_ref):
    sd = jnp.sum(q8d_ref[0].astype(F32) * knd_ref[0].astype(F32), axis=1, keepdims=True)
    sf = jnp.sum(q8f_ref[0].astype(F32) * knf_ref[0].astype(F32), axis=1, keepdims=True)
    ones = jnp.ones((1, LANES), F32)
    vd = vnd_ref[0].astype(F32)
    return (_col_to_periodic(sd), ones, jnp.concatenate([vd, vd], axis=0),
            _col_to_periodic(sf), ones, vnf_ref[0].astype(F32))


def _decode_update(state, bias, d_pages, f_pages, sd_scr, sf_scr):
    md, ld, accd, mf, lf, accf = state
    n_group = len(d_pages)

    p, mf, lf, a_col = _online_flat(sf_scr[...] + bias, mf, lf)
    lhs_a, lhs_b, rhs = [], [], []
    for i in range(n_group):
        pa, pb = _masked_lhs(p[i:i + 1, :])
        lhs_a.append(pa)
        lhs_b.append(pb)
        rhs.append(_pair_lanes(_page_rows(f_pages[i], 1)))
    lhs = jnp.concatenate([jnp.concatenate(lhs_a, axis=1), jnp.concatenate(lhs_b, axis=1)], axis=0)
    o = _dot(lhs.astype(BF16), jnp.concatenate(rhs, axis=0))
    accf = accf * a_col + o[0:N_ROWS, 0:HEAD_DIM] + o[N_ROWS:, HEAD_DIM:]

    p, md, ld, a_col = _online_flat(sd_scr[...], md, ld)
    p_sw = _swap_halves_of_8(p)
    blocks = [[], [], [], []]
    rhs = []
    for i in range(n_group):
        pa, pb = _masked_lhs(p[i:i + 1, :])
        qa, qb = _masked_lhs(p_sw[i:i + 1, :])
        for lst, blk in zip(blocks, (pa, qa, pb, qb)):
            lst.append(blk)
        rhs.append(_pair_lanes(_page_rows(d_pages[i], 1)))
    lhs = jnp.concatenate([jnp.concatenate(b, axis=1) for b in blocks], axis=0)
    o = _dot(lhs.astype(BF16), jnp.concatenate(rhs, axis=0))
    same = o[0:8, 0:HEAD_DIM] + o[16:24, HEAD_DIM:]
    other = o[8:16, 0:HEAD_DIM] + o[24:32, HEAD_DIM:]
    a_sw = pltpu.roll(jnp.broadcast_to(a_col, (N_ROWS, LANES)), N_ROWS // 2, 0)
    accd = jnp.concatenate([accd[0:N_ROWS] * a_col + same, accd[N_ROWS:] * a_sw + other], axis=0)
    return md, ld, accd, mf, lf, accf


def _decode_output(state, lam_ref, g_ref, lam_init):
    md, ld, accd, mf, lf, accf = state
    lp = lam_ref[...]
    lam = (jnp.exp(jnp.sum(lp[0:1] * lp[1:2], axis=1, keepdims=True))
           - jnp.exp(jnp.sum(lp[2:3] * lp[3:4], axis=1, keepdims=True)) + lam_init)
    l_col = _periodic_to_col(ld)
    l_sw = pltpu.roll(jnp.broadcast_to(l_col, (N_ROWS, LANES)), N_ROWS // 2, 0)
    o_same = accd[0:N_ROWS] / l_col
    o_other = accd[N_ROWS:] / l_sw
    row = lax.broadcasted_iota(jnp.int32, (N_ROWS, HEAD_DIM), 0)
    od = jnp.where(row < N_ROWS // 2, o_same - lam * o_other, o_other - lam * o_same)
    ss = jnp.broadcast_to(jnp.sum(od * od, axis=1, keepdims=True), (N_ROWS, LANES))
    ms = (ss + pltpu.roll(ss, N_ROWS // 2, 0)) * (1.0 / (2 * HEAD_DIM))
    od = od * lax.rsqrt(ms + EPS) * g_ref[...] * (1.0 - lam_init)
    return jnp.concatenate([od, accf / _periodic_to_col(lf)], axis=0)


def _merge_kernel(od_ref, of_ref, x_ref, wd_ref, wf_ref, g_ref, o_ref):
    a = _dot(od_ref[...], wd_ref[...]) + _dot(of_ref[...], wf_ref[...])
    o_ref[...] = x_ref[...] + _rms(a, g_ref[...])


def _merge(od, of, x, w_od, w_of, g, tm):
    n, d = x.shape
    half = od.shape[1]
    return pl.pallas_call(
        _merge_kernel,
        grid=(n // tm,),
        in_specs=[
            pl.BlockSpec((tm, half), lambda i: (i, 0)),
            pl.BlockSpec((tm, half), lambda i: (i, 0)),
            pl.BlockSpec((tm, d), lambda i: (i, 0)),
            pl.BlockSpec((half, d), lambda i: (0, 0)),
            pl.BlockSpec((half, d), lambda i: (0, 0)),
            pl.BlockSpec((1, d), lambda i: (0, 0)),
        ],
        out_specs=pl.BlockSpec((tm, d), lambda i: (i, 0)),
        out_shape=jax.ShapeDtypeStruct((n, d), F32),
        compiler_params=_cparams(1),
        name="merge",
    )(od, of, x, w_od, w_of, g)


def _ffn_prologue(x_ref, gpre_ref, h_scr, acc_scr):
    @pl.when(pl.program_id(1) == 0)
    def _():
        h_scr[...] = _rms(x_ref[...], gpre_ref[...]).astype(BF16)
        acc_scr[...] = jnp.zeros_like(acc_scr)


def _ffn_main(wg_ref, wu_ref, wd_ref, h_scr, acc_scr):
    hb = h_scr[...]
    gate = _dot(hb, wg_ref[...])
    up = _dot(hb, wu_ref[...])
    u = gate * jax.nn.sigmoid(gate) * up
    acc_scr[...] += _dot(u.astype(BF16), wd_ref[...])


def _ffn_epilogue(x_ref, gpost_ref, o_ref, acc_scr):
    @pl.when(pl.program_id(1) == pl.num_programs(1) - 1)
    def _():
        o_ref[...] = x_ref[...] + _rms(acc_scr[...], gpost_ref[...])


def _ffn_kernel(x_ref, gpre_ref, wg_ref, wu_ref, wd_ref, gpost_ref, o_ref, h_scr, acc_scr):
    _ffn_prologue(x_ref, gpre_ref, h_scr, acc_scr)
    _ffn_main(wg_ref, wu_ref, wd_ref, h_scr, acc_scr)
    _ffn_epilogue(x_ref, gpost_ref, o_ref, acc_scr)


def _ffn_specs(tm, d, tf, imap):
    return [
        pl.BlockSpec((tm, d), imap(lambda i, j: (i, 0))),
        pl.BlockSpec((1, d), imap(lambda i, j: (0, 0))),
        pl.BlockSpec((d, tf), imap(lambda i, j: (0, j))),
        pl.BlockSpec((d, tf), imap(lambda i, j: (0, j))),
        pl.BlockSpec((tf, d), imap(lambda i, j: (j, 0))),
        pl.BlockSpec((1, d), imap(lambda i, j: (0, 0))),
    ]


def _ffn(x, g_pre, w_gate, w_up, w_down, g_post, tm, tf):
    n, d = x.shape
    f = w_gate.shape[1]
    return pl.pallas_call(
        _ffn_kernel,
        grid=(n // tm, f // tf),
        in_specs=_ffn_specs(tm, d, tf, lambda fn: fn),
        out_specs=pl.BlockSpec((tm, d), lambda i, j: (i, 0)),
        out_shape=jax.ShapeDtypeStruct((n, d), F32),
        scratch_shapes=[pltpu.VMEM((tm, d), BF16), pltpu.VMEM((tm, d), F32)],
        compiler_params=_cparams(2),
        name="ffn",
    )(x, g_pre, w_gate, w_up, w_down, g_post)


def _ffn_decode_kernel(*refs, n_group, steps_per_seq, n_decode_steps, lam_init):
    pt_ref = refs[0]
    x_ref, gpre_ref, wg_ref, wu_ref, wd_ref, gpost_ref = refs[1:7]
    (qqd_ref, qqf_ref, q8d_ref, q8f_ref, knd_ref, knf_ref, vnd_ref, vnf_ref,
     lam_ref, g_ref, bias_ref) = refs[7:18]
    d_pages = refs[18:18 + n_group]
    f_pages = refs[18 + n_group:18 + 2 * n_group]
    y_ref, os_ref = refs[18 + 2 * n_group:20 + 2 * n_group]
    h_scr, acc_scr, sd_scr, sf_scr = refs[20 + 2 * n_group:24 + 2 * n_group]
    state_scr = refs[24 + 2 * n_group:]
    del pt_ref
    t = pl.program_id(0) * pl.num_programs(1) + pl.program_id(1)
    _ffn_prologue(x_ref, gpre_ref, h_scr, acc_scr)

    @pl.when(t == 0)
    def _():
        for scr in state_scr:
            scr[...] = jnp.zeros_like(scr)

    @pl.when(t < n_decode_steps)
    def _():
        _flat_scores(qqf_ref[0], f_pages, sf_scr)
        _flat_scores(qqd_ref[0], d_pages, sd_scr)
        _ffn_main(wg_ref, wu_ref, wd_ref, h_scr, acc_scr)
        first = (t % steps_per_seq) == 0
        cur = tuple(scr[...] for scr in state_scr)
        init = _decode_init(q8d_ref, q8f_ref, knd_ref, knf_ref, vnd_ref, vnf_ref)
        start = tuple(jnp.where(first, a, b) for a, b in zip(init, cur))
        new = _decode_update(start, bias_ref[0, 0], d_pages, f_pages, sd_scr, sf_scr)
        for scr, val in zip(state_scr, new):
            scr[...] = val
        os_ref[0] = _decode_output(new, lam_ref, g_ref, lam_init)

    @pl.when(t >= n_decode_steps)
    def _():
        _ffn_main(wg_ref, wu_ref, wd_ref, h_scr, acc_scr)

    _ffn_epilogue(x_ref, gpost_ref, y_ref, acc_scr)


def _ffn_decode(x, g_pre, w_gate, w_up, w_down, g_post, tm, tf,
                page_table, cache_d, cache_f, bias, qq_d, qq_f, q8_d, q8_f,
                kn_d, kn_f, vn_d, vn_f, lam_params, g_rows, n_group, lam_init):
    n, d = x.shape
    f = w_gate.shape[1]
    nj = f // tf
    db, n_pages = page_table.shape
    sps = n_pages // n_group
    n_dec = db * sps
    assert (n // tm) * nj >= n_dec

    def seq_of(i, j):
        t = jnp.minimum(i * nj + j, n_dec - 1)
        return t // sps, t % sps

    def seq3(shape):
        return pl.BlockSpec((1,) + shape, lambda i, j, pt: (seq_of(i, j)[0], 0, 0))

    def page_spec(k):
        def imap(i, j, pt):
            b, g = seq_of(i, j)
            return (0, pt[b, g * n_group + k], 0, 0, 0, 0)
        return pl.BlockSpec((1, 1, PAGE_SIZE, 2, N_ROWS, HEAD_DIM), imap)

    in_specs = _ffn_specs(tm, d, tf, lambda fn: (lambda i, j, pt: fn(i, j)))
    in_specs += [
        seq3((2 * N_ROWS, 2 * HEAD_DIM)), seq3((2 * N_ROWS, 2 * HEAD_DIM)),
        seq3((N_ROWS, HEAD_DIM)), seq3((N_ROWS, HEAD_DIM)),
        seq3((N_ROWS, HEAD_DIM)), seq3((N_ROWS, HEAD_DIM)),
        seq3((N_ROWS, HEAD_DIM)), seq3((N_ROWS, HEAD_DIM)),
        pl.BlockSpec((4, HEAD_DIM), lambda i, j, pt: (0, 0)),
        pl.BlockSpec((N_ROWS, HEAD_DIM), lambda i, j, pt: (0, 0)),
        pl.BlockSpec((1, 1, n_group, FLAT), lambda i, j, pt: seq_of(i, j) + (0, 0)),
    ]
    in_specs += [page_spec(k) for k in range(n_group)] * 2
    return pl.pallas_call(
        functools.partial(_ffn_decode_kernel, n_group=n_group, steps_per_seq=sps,
                          n_decode_steps=n_dec, lam_init=lam_init),
        grid_spec=pltpu.PrefetchScalarGridSpec(
            num_scalar_prefetch=1,
            grid=(n // tm, nj),
            in_specs=in_specs,
            out_specs=[
                pl.BlockSpec((tm, d), lambda i, j, pt: (i, 0)),
                pl.BlockSpec((1, 2 * N_ROWS, HEAD_DIM), lambda i, j, pt: (seq_of(i, j)[0], 0, 0)),
            ],
            scratch_shapes=[
                pltpu.VMEM((tm, d), BF16), pltpu.VMEM((tm, d), F32),
                pltpu.VMEM((n_group, FLAT), F32), pltpu.VMEM((n_group, FLAT), F32),
                pltpu.VMEM((1, LANES), F32), pltpu.VMEM((1, LANES), F32),
                pltpu.VMEM((2 * N_ROWS, HEAD_DIM), F32),
                pltpu.VMEM((1, LANES), F32), pltpu.VMEM((1, LANES), F32),
                pltpu.VMEM((N_ROWS, HEAD_DIM), F32),
            ],
        ),
        out_shape=[
            jax.ShapeDtypeStruct((n, d), F32),
            jax.ShapeDtypeStruct((db, 2 * N_ROWS, HEAD_DIM), F32),
        ],
        compiler_params=_cparams(2, LARGE_VMEM_LIMIT),
        name="ffn_decode",
    )(page_table, x, g_pre, w_gate, w_up, w_down, g_post,
      qq_d, qq_f, q8_d, q8_f, kn_d, kn_f, vn_d, vn_f, lam_params, g_rows,
      bias.reshape(db, sps, n_group, FLAT), *([cache_d] * n_group), *([cache_f] * n_group))


def _proj_tables(pos, scale):
    half = HEAD_DIM // 2
    inv = ROPE_THETA ** (-jnp.arange(half, dtype=F32) / half)
    ang = pos.astype(F32)[:, None] * inv[None, :]
    cos, sin = jnp.cos(ang), jnp.sin(ang)
    cos, sin = jnp.concatenate([cos, cos], axis=1), jnp.concatenate([-sin, sin], axis=1)
    one, zero = jnp.ones_like(cos), jnp.zeros_like(cos)
    return (jnp.stack([cos * scale, cos, one, one * scale, one, one]),
            jnp.stack([sin * scale, sin, zero, zero, zero, zero]))


def _tile_rows(n, target):
    t = min(n, target)
    while n % t:
        t //= 2
    return t


def _as_key_tiles(cache):
    _, pool, page, _, rows, width = cache.shape
    halves = width // HEAD_DIM
    c = cache.reshape(1, pool, page, 2, rows, halves, HEAD_DIM)
    return jnp.transpose(c, (0, 1, 2, 3, 5, 4, 6)).reshape(1, pool, page, 2, rows * halves, HEAD_DIM)


def _rows_half_head(v):
    n = v.shape[0]
    return jnp.transpose(v.reshape(n, 4, 2, HEAD_DIM), (0, 2, 1, 3)).reshape(n, N_ROWS, HEAD_DIM)


def kernel(x_prompt, x_sample, cache_diff_kv, cache_fox_kv, cache_fox_logf, page_table, meta_tokens, g_attn_pre, w_in, b_f, diff_lambda, g_subln, w_o, g_attn_post, g_ffn_pre, w_gate, w_up, w_down, g_ffn_post):
    batch, seq, d_model = x_prompt.shape
    db, t_new, _ = x_sample.shape
    assert t_new == 1 and w_in.shape[0] == 1
    n_pages = page_table.shape[1]
    past = n_pages * PAGE_SIZE
    lam_init = 0.8 - 0.6 * math.exp(-0.3 * 0)
    n_p = batch * seq
    n_main = 6 * HEAD_COLS

    w_main = w_in[0][:, :n_main].astype(BF16)
    w_fl = jnp.pad(w_in[0][:, n_main:], ((0, 0), (0, LANES - N_ROWS))).astype(BF16)
    b_fl = jnp.pad(b_f[0].astype(F32), (0, LANES - N_ROWS))[None, :]
    g_pre = g_attn_pre[0][None, :].astype(F32)

    tm_p = _tile_rows(seq, 512)
    xp = x_prompt.reshape(n_p, d_model)
    scale = HEAD_DIM ** -0.5
    tab_a, tab_b = _proj_tables(N_META + jnp.arange(seq), scale * LOG2E)
    qkv_p, kvd_p, kvf_p, lf_p = _proj(xp, g_pre, w_main, w_fl, b_fl, tab_a, tab_b, tm_p)

    n_e = N_META + db
    x_e = jnp.concatenate([meta_tokens.astype(F32), x_sample.reshape(db, d_model)], axis=0)
    pos_e = jnp.concatenate([jnp.arange(N_META), jnp.full((db,), past)])
    tab_a, tab_b = _proj_tables(pos_e, scale)
    qkv_e, kvd_e, kvf_e, lf_e = _proj(x_e, g_pre, w_main, w_fl, b_fl, tab_a, tab_b, n_e)

    tc = _tile_rows(seq, 256)
    dcum_p, dcum_m = _cumsum(lf_p, lf_e[:N_META], batch, tc, LOG2E)
    tq = _tile_rows(seq, 512)
    tk = tq
    ck = jnp.transpose(dcum_p[:, :N_ROWS].reshape(batch, seq, N_ROWS), (0, 2, 1))
    ck = ck.reshape(batch, N_ROWS, seq // tk, tk)
    ckm = jnp.pad(dcum_m[:N_META, :N_ROWS].T, ((0, 0), (0, LANES - N_META)))
    qkv_m = jnp.pad(qkv_e[:N_META], ((0, LANES - N_META), (0, 0)))
    lam_params = diff_lambda[0].astype(F32)
    g_sub = g_subln[0].astype(F32)[None, :]
    od_p, of_p = _prompt_attention(qkv_p, qkv_m, dcum_p, ck, ckm, lam_params, g_sub,
                                   batch, seq, tq, tk, lam_init)

    qkv_s = qkv_e[N_META:]
    lf_s = lf_e[N_META:, :N_ROWS]
    cols = lambda a, j: a[:, j * HEAD_COLS:(j + 1) * HEAD_COLS]
    q8_d = _rows_half_head(cols(qkv_s, 0))
    kn_d = _rows_half_head(cols(qkv_s, 1))
    vn_d = _rows_half_head(cols(qkv_s, 2))
    q8_f = cols(qkv_s, 3).reshape(db, N_ROWS, HEAD_DIM)
    kn_f = cols(qkv_s, 4).reshape(db, N_ROWS, HEAD_DIM)
    vn_f = cols(qkv_s, 5).reshape(db, N_ROWS, HEAD_DIM)

    def paired(q8):
        z = jnp.zeros_like(q8)
        return jnp.concatenate([jnp.concatenate([q8, z], axis=2), jnp.concatenate([z, q8], axis=2)], axis=1)

    lf_flat = cache_fox_logf[0][page_table].astype(F32).reshape(db, n_pages, FLAT)
    ct_flat = jnp.tile(lf_s, (1, PAGE_SIZE))[:, None, :]
    bias = _sample_bias(lf_flat, ct_flat)
    g_rows = jnp.repeat(g_subln[0].astype(F32).reshape(2, 1, HEAD_DIM), 4, axis=1).reshape(N_ROWS, HEAD_DIM)
    n_group = 4 if n_pages % 4 == 0 else 1

    w_od = w_o[0][:HEAD_COLS].astype(BF16)
    w_of = w_o[0][HEAD_COLS:].astype(BF16)
    g_post = g_attn_post[0][None, :].astype(F32)
    g_fpre = g_ffn_pre[0][None, :].astype(F32)
    g_fpost = g_ffn_post[0][None, :].astype(F32)
    wg, wu, wd = w_gate[0].astype(BF16), w_up[0].astype(BF16), w_down[0].astype(BF16)
    x1_p = _merge(od_p, of_p, xp, w_od, w_of, g_post, _tile_rows(n_p, 256))
    y_p, o_s = _ffn_decode(x1_p, g_fpre, wg, wu, wd, g_fpost, _tile_rows(n_p, 512), 256,
                           page_table, _as_key_tiles(cache_diff_kv), _as_key_tiles(cache_fox_kv), bias,
                           paired(q8_d), paired(q8_f), q8_d, q8_f, kn_d, kn_f, vn_d, vn_f,
                           lam_params, g_rows, n_group, lam_init)
    od_s = jnp.transpose(o_s[:, :N_ROWS].reshape(db, 2, 4, HEAD_DIM), (0, 2, 1, 3)).reshape(db, HEAD_COLS)
    of_s = o_s[:, N_ROWS:].reshape(db, HEAD_COLS)
    xs = x_sample.reshape(db, d_model)
    x1_s = _merge(od_s.astype(BF16), of_s.astype(BF16), xs, w_od, w_of, g_post, db)
    y_s = _ffn(x1_s, g_fpre, wg, wu, wd, g_fpost, db, 512)

    def with_meta(rows_e, rows_p, tail):
        m = jnp.broadcast_to(rows_e[:N_META][None], (batch, N_META) + rows_e.shape[1:])
        full = jnp.concatenate([m, rows_p.reshape((batch, seq) + rows_p.shape[1:])], axis=1)
        return full.reshape((1, batch, N_META + seq) + tail)

    dkv_p = with_meta(kvd_e, kvd_p, (2, 4, 2 * HEAD_DIM)).astype(cache_diff_kv.dtype)
    fkv_p = with_meta(kvf_e, kvf_p, (2, N_ROWS, HEAD_DIM)).astype(cache_fox_kv.dtype)
    flf_p = with_meta(lf_e[:, :N_ROWS], lf_p[:, :N_ROWS], (N_ROWS,)).astype(cache_fox_logf.dtype)
    dkv_s = kvd_e[N_META:].reshape(1, db, 1, 2, 4, 2 * HEAD_DIM).astype(cache_diff_kv.dtype)
    fkv_s = kvf_e[N_META:].reshape(1, db, 1, 2, N_ROWS, HEAD_DIM).astype(cache_fox_kv.dtype)
    flf_s = lf_s.reshape(1, db, 1, N_ROWS).astype(cache_fox_logf.dtype)
    return (y_p.reshape(batch, seq, d_model), y_s.reshape(db, 1, d_model),
            dkv_p, fkv_p, flf_p, dkv_s, fkv_s, flf_s)
```
